```python
import math
import jax, jax.numpy as jnp
from jax import lax
import numpy as np


D_MODEL = 2048
BATCH = 4
SEQ = 8192
DEPTH = 2
DEC_BATCH = 8
DEC_SEQ = 16
PAST_LEN = 2048

CHUNK = 64
GROUP_WIDTH = D_MODEL // 4
MIX_WIDTH = 4 * GROUP_WIDTH
GMLP_CHUNK = 128
A_HEADS = 4
A_HEAD_DIM = GROUP_WIDTH // A_HEADS
B_CONV_WIDTH = 31
C_HEAD_DIM = 64
C_HEADS = GROUP_WIDTH // C_HEAD_DIM
C_GROUPS = 2
C_STATE = 128
C_CONV_WIDTH = 4
C_XBC = GROUP_WIDTH + 2 * C_GROUPS * C_STATE
SSD_BLOCK = CHUNK
D_CONV_WIDTH = 3
D_FF = 5632
IN_A = 2 * GROUP_WIDTH
IN_B = 2 * GROUP_WIDTH
IN_C = GROUP_WIDTH + C_XBC + C_HEADS
IN_D = 3 * GROUP_WIDTH
IN_WIDTH = IN_A + IN_B + IN_C + IN_D
NORM_EPS = 1e-6
F32 = jnp.float32

kernel_name = 'hybrid_parallel_group_streaming_encoder_step'


def rms_norm(x, g):
    xf = x.astype(F32)
    y = xf * lax.rsqrt(jnp.mean(xf * xf, axis=-1, keepdims=True) + NORM_EPS)
    return (y * g.astype(F32)).astype(x.dtype)


def layer_norm(x, g, b):
    xf = x.astype(F32)
    mu = jnp.mean(xf, axis=-1, keepdims=True)
    xc = xf - mu
    y = xc * lax.rsqrt(jnp.mean(xc * xc, axis=-1, keepdims=True) + NORM_EPS)
    return (y * g.astype(F32) + b.astype(F32)).astype(x.dtype)


def swiglu(h, w_in, w_out):
    g, u = jnp.split(h @ w_in, 2, axis=-1)
    return (jax.nn.silu(g) * u) @ w_out


def causal_dwconv(x, buf, w, b):
    K, C = w.shape
    xp = jnp.concatenate([buf.astype(x.dtype), x], axis=1)
    y = lax.conv_general_dilated(xp, w[:, None, :].astype(x.dtype), window_strides=(1,), padding='VALID',
                                 dimension_numbers=('NWC', 'WIO', 'NWC'), feature_group_count=C)
    if b is not None:
        y = y + b.astype(x.dtype)
    return y, xp[:, -(K - 1):]


def gmlp_mix(p, ln_g, ln_b, ws, bs):
    n, L, _ = p.shape
    u, v = jnp.split(jax.nn.gelu(p, approximate=False), 2, axis=-1)
    v = layer_norm(v, ln_g, ln_b)
    T = min(GMLP_CHUNK, L)
    vh = v.reshape(n, L // T, T, A_HEADS, A_HEAD_DIM)
    mask = jnp.tril(jnp.ones((T, T), bool))
    w = jnp.where(mask, ws[:, :T, :T], 0).astype(v.dtype)
    s = jnp.einsum('hts,bcshd->bcthd', w, vh) + bs[:, :T].T[None, None, :, :, None].astype(v.dtype)
    return u * s.reshape(n, L, GROUP_WIDTH), v


def conformer_conv(p, buf, w, b, ln_g, ln_b):
    a, g = jnp.split(p, 2, axis=-1)
    y, new_buf = causal_dwconv(a * jax.nn.sigmoid(g), buf, w, b)
    return jax.nn.silu(layer_norm(y, ln_g, ln_b)), new_buf


def ssd_scan(x, dt, A, Bm, Cm, h0, block):
    nb, L, H, P = x.shape
    G, K = Bm.shape[2], Bm.shape[3]
    R = H // G
    nc = L // block
    xc = x.reshape(nb, nc, block, G, R, P)
    dtc = dt.reshape(nb, nc, block, G, R)
    Bc = Bm.reshape(nb, nc, block, G, K)
    Cc = Cm.reshape(nb, nc, block, G, K)
    acum = jnp.cumsum(dtc * A.reshape(G, R), axis=2)
    xdt = xc * dtc[..., None]
    seg = acum[:, :, :, None] - acum[:, :, None, :]
    causal = jnp.tril(jnp.ones((block, block), bool))[:, :, None, None]
    decay = jnp.exp(jnp.where(causal, seg, -jnp.inf))
    cb = jnp.einsum('bctgk,bcsgk->bctsg', Cc, Bc)
    y_intra = jnp.einsum('bctsgr,bcsgrp->bctgrp', cb[..., None] * decay, xdt)
    last = acum[:, :, -1]
    to_end = jnp.exp(last[:, :, None] - acum)
    s_block = jnp.einsum('bcsgk,bcsgrp->bcgrpk', Bc, xdt * to_end[..., None])

    def step(h, inp):
        dec, s = inp
        return dec[..., None, None] * h + s, h

    h_last, h_prev = lax.scan(step, h0.reshape(nb, G, R, P, K),
                              (jnp.moveaxis(jnp.exp(last), 1, 0), jnp.moveaxis(s_block, 1, 0)))
    h_prev = jnp.moveaxis(h_prev, 0, 1)
    y_inter = jnp.einsum('bctgk,bcgrpk->bctgrp', Cc, h_prev) * jnp.exp(acum)[..., None]
    y = (y_intra + y_inter).reshape(nb, L, H, P)
    return y, h_last.reshape(nb, H, P, K)


def mamba2_mix(p, conv_buf, h0, conv_w, conv_b, dt_bias, a_log, d_skip, norm_g):
    n, L, _ = p.shape
    z, xbc, dt = jnp.split(p, [GROUP_WIDTH, GROUP_WIDTH + C_XBC], axis=-1)
    xbc, new_buf = causal_dwconv(xbc, conv_buf, conv_w, conv_b)
    xbc = jax.nn.silu(xbc).astype(F32)
    xs, bm, cm = jnp.split(xbc, [GROUP_WIDTH, GROUP_WIDTH + C_GROUPS * C_STATE], axis=-1)
    xs = xs.reshape(n, L, C_HEADS, C_HEAD_DIM)
    bm = bm.reshape(n, L, C_GROUPS, C_STATE)
    cm = cm.reshape(n, L, C_GROUPS, C_STATE)
    dt = jax.nn.softplus(dt.astype(F32) + dt_bias.astype(F32))
    A = -jnp.exp(a_log.astype(F32))
    y, h = ssd_scan(xs, dt, A, bm, cm, h0.astype(F32), min(SSD_BLOCK, L))
    y = y + d_skip.astype(F32)[:, None] * xs
    gsz = GROUP_WIDTH // C_GROUPS
    y = y.reshape(n, L, C_GROUPS, gsz) * jax.nn.silu(z.astype(F32)).reshape(n, L, C_GROUPS, gsz)
    y = y * lax.rsqrt(jnp.mean(y * y, axis=-1, keepdims=True) + NORM_EPS)
    y = y.reshape(n, L, GROUP_WIDTH) * norm_g.astype(F32)
    return y.astype(p.dtype), new_buf, h.astype(h0.dtype)


def short_conv_mix(p, buf, w):
    bg, cg, xx = jnp.split(p, 3, axis=-1)
    y, new_buf = causal_dwconv(cg * xx, buf, w, None)
    return bg * y, new_buf


def layer(x, st_b, st_c, st_ssm, st_d, lw):
    (norm_g, ffn_w_in, ffn_w_out, w_in, w_out, a_ln_g, a_ln_b, a_ws, a_bs,
     b_conv_w, b_conv_b, b_ln_g, b_ln_b, c_conv_w, c_conv_b, c_dt_bias, c_a_log, c_d, c_norm_g,
     d_conv_w) = lw
    x = x + 0.5 * rms_norm(swiglu(rms_norm(x, norm_g[0]), ffn_w_in[0], ffn_w_out[0]), norm_g[1])
    h = rms_norm(x, norm_g[2])
    pa, pb, pc, pd = jnp.split(h @ w_in, [IN_A, IN_A + IN_B, IN_A + IN_B + IN_C], axis=-1)
    ya, va = gmlp_mix(pa, a_ln_g, a_ln_b, a_ws, a_bs)
    yb, nb = conformer_conv(pb, st_b, b_conv_w, b_conv_b, b_ln_g, b_ln_b)
    yc, nc, nh = mamba2_mix(pc, st_c, st_ssm, c_conv_w, c_conv_b, c_dt_bias, c_a_log, c_d, c_norm_g)
    yd, nd = short_conv_mix(pd, st_d, d_conv_w)
    mix = jnp.concatenate([ya, yb, yc, yd], axis=-1) @ w_out
    x = x + rms_norm(mix, norm_g[3])
    x = x + 0.5 * rms_norm(swiglu(rms_norm(x, norm_g[4]), ffn_w_in[1], ffn_w_out[1]), norm_g[5])
    return x, (va, nb, nc, nh, nd)


def trunk(x, st_b, st_c, st_ssm, st_d, weights):
    new_states = []
    for l in range(DEPTH):
        lw = tuple(w[l] for w in weights)
        x, st = layer(x, st_b[l], st_c[l], st_ssm[l], st_d[l], lw)
        new_states.append(st)
    v, nb, nc, nh, nd = [jnp.stack(s) for s in zip(*new_states)]
    return x, v, nb, nc, nh, nd


def setup_inputs(seed: int = 0) -> dict:
    key = jax.random.key(seed)
    ks = jax.random.split(key, 26)

    def nrm(k, shape, scale=1.0):
        return jax.random.normal(k, shape, F32) * scale

    dt0 = jnp.exp(jax.random.uniform(ks[20], (DEPTH, C_HEADS), F32, math.log(1e-3), math.log(1e-1)))
    dt_bias = dt0 + jnp.log(-jnp.expm1(-dt0))
    return {
        'x_prompt': nrm(ks[0], (BATCH, SEQ, D_MODEL)),
        'x_sample': nrm(ks[1], (DEC_BATCH, DEC_SEQ, D_MODEL)),
        'cache_conv_b': nrm(ks[2], (DEPTH, DEC_BATCH, B_CONV_WIDTH - 1, GROUP_WIDTH)),
        'cache_conv_c': nrm(ks[3], (DEPTH, DEC_BATCH, C_CONV_WIDTH - 1, C_XBC)),
        'state_ssm': nrm(ks[4], (DEPTH, DEC_BATCH, C_HEADS, C_HEAD_DIM, C_STATE), 0.5),
        'cache_conv_d': nrm(ks[5], (DEPTH, DEC_BATCH, D_CONV_WIDTH - 1, GROUP_WIDTH)),
        'norm_g': 1.0 + nrm(ks[6], (DEPTH, 6, D_MODEL), 0.05),
        'ffn_w_in': nrm(ks[7], (DEPTH, 2, D_MODEL, 2 * D_FF), D_MODEL ** -0.5),
        'ffn_w_out': nrm(ks[8], (DEPTH, 2, D_FF, D_MODEL), D_FF ** -0.5),
        'w_in': nrm(ks[9], (DEPTH, D_MODEL, IN_WIDTH), D_MODEL ** -0.5),
        'w_out': nrm(ks[10], (DEPTH, MIX_WIDTH, D_MODEL), MIX_WIDTH ** -0.5),
        'a_ln_g': 1.0 + nrm(ks[11], (DEPTH, GROUP_WIDTH), 0.05),
        'a_ln_b': nrm(ks[12], (DEPTH, GROUP_WIDTH), 0.01),
        'a_ws': nrm(ks[13], (DEPTH, A_HEADS, GMLP_CHUNK, GMLP_CHUNK), GMLP_CHUNK ** -0.5),
        'a_bs': 1.0 + nrm(ks[14], (DEPTH, A_HEADS, GMLP_CHUNK), 0.1),
        'b_conv_w': nrm(ks[15], (DEPTH, B_CONV_WIDTH, GROUP_WIDTH), B_CONV_WIDTH ** -0.5),
        'b_conv_b': nrm(ks[16], (DEPTH, GROUP_WIDTH), 0.01),
        'b_ln_g': 1.0 + nrm(ks[17], (DEPTH, GROUP_WIDTH), 0.05),
        'b_ln_b': nrm(ks[18], (DEPTH, GROUP_WIDTH), 0.01),
        'c_conv_w': nrm(ks[19], (DEPTH, C_CONV_WIDTH, C_XBC), C_CONV_WIDTH ** -0.5),
        'c_conv_b': nrm(ks[21], (DEPTH, C_XBC), 0.01),
        'c_dt_bias': dt_bias,
        'c_a_log': jnp.log(jax.random.uniform(ks[22], (DEPTH, C_HEADS), F32, 1.0, 16.0)),
        'c_d': 1.0 + nrm(ks[23], (DEPTH, C_HEADS), 0.1),
        'c_norm_g': 1.0 + nrm(ks[24], (DEPTH, GROUP_WIDTH), 0.05),
        'd_conv_w': nrm(ks[25], (DEPTH, D_CONV_WIDTH, GROUP_WIDTH), D_CONV_WIDTH ** -0.5),
    }


def reference(x_prompt, x_sample, cache_conv_b, cache_conv_c, state_ssm, cache_conv_d,
              norm_g, ffn_w_in, ffn_w_out, w_in, w_out, a_ln_g, a_ln_b, a_ws, a_bs,
              b_conv_w, b_conv_b, b_ln_g, b_ln_b, c_conv_w, c_conv_b, c_dt_bias, c_a_log, c_d, c_norm_g,
              d_conv_w):
    weights = (norm_g, ffn_w_in, ffn_w_out, w_in, w_out, a_ln_g, a_ln_b, a_ws, a_bs,
               b_conv_w, b_conv_b, b_ln_g, b_ln_b, c_conv_w, c_conv_b, c_dt_bias, c_a_log, c_d, c_norm_g,
               d_conv_w)
    nb = x_prompt.shape[0]
    dt = x_prompt.dtype
    zb = jnp.zeros((DEPTH, nb, B_CONV_WIDTH - 1, GROUP_WIDTH), dt)
    zc = jnp.zeros((DEPTH, nb, C_CONV_WIDTH - 1, C_XBC), dt)
    zh = jnp.zeros((DEPTH, nb, C_HEADS, C_HEAD_DIM, C_STATE), dt)
    zd = jnp.zeros((DEPTH, nb, D_CONV_WIDTH - 1, GROUP_WIDTH), dt)
    y_prompt, _, p_conv_b, p_conv_c, p_ssm, p_conv_d = trunk(x_prompt, zb, zc, zh, zd, weights)
    y_sample, s_gmlp_v, s_conv_b, s_conv_c, s_ssm, s_conv_d = trunk(
        x_sample, cache_conv_b, cache_conv_c, state_ssm, cache_conv_d, weights)
    return (y_prompt, y_sample, p_conv_b, p_conv_c, p_ssm, p_conv_d,
            s_gmlp_v, s_conv_b, s_conv_c, s_ssm, s_conv_d)
```

```python
import functools
import math

import numpy as np
import jax
import jax.numpy as jnp
from jax import lax
from jax.experimental import pallas as pl
from jax.experimental.pallas import tpu as pltpu

F32 = jnp.float32
BF16 = jnp.bfloat16

D_MODEL = 2048
D_FF = 5632
GROUP_WIDTH = 512
A_HEADS = 4
GMLP_CHUNK = 128
B_CONV_WIDTH = 31
C_HEADS = 8
C_HEAD_DIM = 64
C_STATE = 128
C_CONV_WIDTH = 4
C_XBC = 1024
D_CONV_WIDTH = 3
SSD_BLOCK = 64
IN_WIDTH = 5128
IN_D_START = 3592
NORM_EPS = 1e-6
INV_SQRT2 = float(1.0 / np.float32(np.sqrt(2.0)))

SUBLANES = 8
LANES = 128
HALO_B = 32
HALO_CD = 8
VMEM_LIMIT_BYTES = 56 * 1024 * 1024


def _rms(x, g):
    ms = jnp.mean(x * x, axis=-1, keepdims=True)
    return x * lax.rsqrt(ms + NORM_EPS) * g


def _layer_norm(x, g, b):
    mu = jnp.mean(x, axis=-1, keepdims=True)
    xc = x - mu
    var = jnp.mean(xc * xc, axis=-1, keepdims=True)
    return xc * lax.rsqrt(var + NORM_EPS) * g + b


def _silu(x):
    return x * jax.nn.sigmoid(x)


def _dot(a, b):
    return jnp.dot(a, b, preferred_element_type=F32)


def _dot_nt(a, b):
    return lax.dot_general(a, b, (((1,), (1,)), ((), ())), preferred_element_type=F32)


def _ffn_body(nk, x_ref, gin_ref, wg_ref, wu_ref, wo_ref, gout_ref, o_ref, h_scr):
    k = pl.program_id(1)

    @pl.when(k == 0)
    def _():
        h_scr[...] = _rms(x_ref[...], gin_ref[...]).astype(BF16)
        o_ref[...] = jnp.zeros_like(o_ref)

    h = h_scr[...]
    g = _dot(h, wg_ref[...])
    u = _dot(h, wu_ref[...])
    a = (_silu(g) * u).astype(BF16)
    o_ref[...] += _dot(a, wo_ref[...])

    @pl.when(k == nk - 1)
    def _():
        o_ref[...] = x_ref[...] + 0.5 * _rms(o_ref[...], gout_ref[...])


def _ffn(x2d, normg, w_in_bf, w_out_bf, layer, which, tm, tf=512):
    m = x2d.shape[0]
    nk = D_FF // tf
    gi = layer * 6 + (0 if which == 0 else 4)
    go = gi + 1
    return pl.pallas_call(
        functools.partial(_ffn_body, nk),
        grid=(m // tm, nk),
        in_specs=[
            pl.BlockSpec((tm, D_MODEL), lambda i, k: (i, 0)),
            pl.BlockSpec((None, 1, D_MODEL), lambda i, k: (gi, 0, 0)),
            pl.BlockSpec((None, None, D_MODEL, tf), lambda i, k: (layer, which, 0, k)),
            pl.BlockSpec((None, None, D_MODEL, tf), lambda i, k: (layer, which, 0, k + nk)),
            pl.BlockSpec((None, None, tf, D_MODEL), lambda i, k: (layer, which, k, 0)),
            pl.BlockSpec((None, 1, D_MODEL), lambda i, k: (go, 0, 0)),
        ],
        out_specs=pl.BlockSpec((tm, D_MODEL), lambda i, k: (i, 0)),
        out_shape=jax.ShapeDtypeStruct((m, D_MODEL), F32),
        scratch_shapes=[pltpu.VMEM((tm, D_MODEL), BF16)],
        compiler_params=pltpu.CompilerParams(
            dimension_semantics=("parallel", "arbitrary"),
            vmem_limit_bytes=VMEM_LIMIT_BYTES),
        name=f"ffn_l{layer}_{which}",
    )(x2d, normg, w_in_bf, w_in_bf, w_out_bf, normg)


def _dwconv(ext_ref, w_ref, b_ref, out_ref, taps, halo, tm, width):
    rows = 64
    for r0 in range(0, tm, rows):
        for c0 in range(0, width, LANES):
            base = halo - (taps - 1) + r0
            acc = w_ref[0:1, c0:c0 + LANES] * ext_ref[base:base + rows, c0:c0 + LANES]
            for k in range(1, taps):
                acc = acc + w_ref[k:k + 1, c0:c0 + LANES] * ext_ref[base + k:base + k + rows, c0:c0 + LANES]
            if b_ref is not None:
                acc = acc + b_ref[0:1, c0:c0 + LANES]
            out_ref[r0:r0 + rows, c0:c0 + LANES] = acc


def _mix_body(tm, valid, emit_v, nt, *refs):
    (x_ref, g2_ref, g3_ref, wa_ref, wb_ref, wz_ref, wxs_ref, wbc_ref, wdt_ref, wd_ref, wo_ref,
     alg_ref, alb_ref, aws_ref, absT_ref,
     bcw_ref, bcb_ref, blg_ref, blb_ref,
     ccw_ref, ccb_ref, cdtb_ref, calog_ref, cdsk_ref, cng_ref,
     dcw_ref,
     stb_ref, stc_ref, sth_ref, std_ref) = refs[:30]
    n_out = 6 if emit_v else 5
    outs = refs[30:30 + n_out]
    y_ref, nb_ref, nc_ref, nh_ref, nd_ref = outs[:5]
    v_ref = outs[5] if emit_v else None
    extb, extc, extd, s_scr, tmp, yc, mix, dt_scr, a_scr = refs[30 + n_out:]

    t = pl.program_id(1)

    @pl.when(t == 0)
    def _():
        extb[0:HALO_B, :] = stb_ref[...]
        extc[0:HALO_CD, :] = stc_ref[...]
        extd[0:HALO_CD, :] = std_ref[...]
        for j in range(4):
            s_scr[j] = sth_ref[j].T

    x = x_ref[...]
    h = _rms(x, g2_ref[...]).astype(BF16)

    pa = _dot(h, wa_ref[...])
    ga = pa * (lax.erf(pa * INV_SQRT2) + 1.0) * 0.5
    u = ga[:, :GROUP_WIDTH]
    vn = _layer_norm(ga[:, GROUP_WIDTH:], alg_ref[...], alb_ref[...])
    if emit_v:
        v_ref[...] = vn
    vb = vn.astype(BF16)
    tc = min(GMLP_CHUNK, tm)
    rr = lax.broadcasted_iota(jnp.int32, (tc, tc), 0)
    cc = lax.broadcasted_iota(jnp.int32, (tc, tc), 1)
    for hd in range(A_HEADS):
        w = jnp.where(rr >= cc, aws_ref[hd, 0:tc, 0:tc], 0.0).astype(BF16)
        bcol = absT_ref[0:tc, hd:hd + 1]
        for c in range(tm // tc):
            rs = slice(c * tc, (c + 1) * tc)
            ls = slice(hd * LANES, (hd + 1) * LANES)
            s = _dot(w, vb[rs, ls]) + bcol
            mix[rs, ls] = (u[rs, ls] * s).astype(BF16)

    pb = _dot(h, wb_ref[...])
    extb[HALO_B:HALO_B + tm, :] = pb[:, :GROUP_WIDTH] * jax.nn.sigmoid(pb[:, GROUP_WIDTH:])
    _dwconv(extb, bcw_ref, bcb_ref, tmp, B_CONV_WIDTH, HALO_B, tm, GROUP_WIDTH)
    nb_ref[...] = extb[valid:valid + HALO_B, :]
    extb[0:HALO_B, :] = extb[tm:tm + HALO_B, :]
    yb = _silu(_layer_norm(tmp[:, 0:GROUP_WIDTH], blg_ref[...], blb_ref[...]))
    mix[:, GROUP_WIDTH:2 * GROUP_WIDTH] = yb.astype(BF16)

    pz = _dot(h, wz_ref[...])
    extc[HALO_CD:HALO_CD + tm, 0:GROUP_WIDTH] = _dot(h, wxs_ref[...])
    extc[HALO_CD:HALO_CD + tm, GROUP_WIDTH:C_XBC] = _dot(h, wbc_ref[...])
    pdt = _dot(h, wdt_ref[...])
    _dwconv(extc, ccw_ref, ccb_ref, tmp, C_CONV_WIDTH, HALO_CD, tm, C_XBC)
    nc_ref[...] = extc[valid:valid + HALO_CD, :]
    extc[0:HALO_CD, :] = extc[tm:tm + HALO_CD, :]
    tmp[...] = _silu(tmp[...])

    dpre = pdt + cdtb_ref[...]
    dtv = jnp.maximum(dpre, 0.0) + jnp.log1p(jnp.exp(-jnp.abs(dpre)))
    if valid < tm:
        rowid = lax.broadcasted_iota(jnp.int32, (tm, LANES), 0)
        dtv = jnp.where(rowid < valid, dtv, 0.0)
    dt_scr[...] = dtv
    a_scr[...] = dtv * (-jnp.exp(calog_ref[...]))

    bs = SSD_BLOCK
    tri = (lax.broadcasted_iota(jnp.int32, (bs, bs), 0)
           >= lax.broadcasted_iota(jnp.int32, (bs, bs), 1)).astype(F32)
    lane = lax.broadcasted_iota(jnp.int32, (bs, LANES), 1)
    lane1 = lax.broadcasted_iota(jnp.int32, (1, LANES), 1)
    lane2 = lax.broadcasted_iota(jnp.int32, (2 * bs, LANES), 1)
    row2 = lax.broadcasted_iota(jnp.int32, (2 * bs, LANES), 0)
    blockdiag = ((row2 < bs) & (lane2 < bs)) | ((row2 >= bs) & (lane2 >= bs))
    left = lane < bs
    left1 = lane1 < bs
    causal2 = lax.broadcasted_iota(jnp.int32, (bs, LANES), 0) >= jnp.where(left, lane, lane - bs)
    zblk = jnp.zeros((bs, LANES), F32)

    for c in range(tm // bs):
        rs = slice(c * bs, (c + 1) * bs)
        acum = lax.dot_general(tri, a_scr[rs, :], (((1,), (0,)), ((), ())),
                               precision=lax.Precision.HIGHEST, preferred_element_type=F32)
        dt_blk = dt_scr[rs, :]
        ac_t = jnp.concatenate([acum, acum], axis=0).T
        last = acum[bs - 1:bs, :]
        for g in range(2):
            bg_ = tmp[rs, GROUP_WIDTH + g * C_STATE:GROUP_WIDTH + (g + 1) * C_STATE]
            cg_ = tmp[rs, GROUP_WIDTH + (2 + g) * C_STATE:GROUP_WIDTH + (3 + g) * C_STATE].astype(BF16)
            cb2 = _dot_nt(cg_, jnp.concatenate([bg_, bg_], axis=0).astype(BF16))
            bg_t = jnp.concatenate([bg_, zblk], axis=0).T.astype(BF16)
            for jj in range(2):
                j = g * 2 + jj
                ha, hb = 2 * j, 2 * j + 1
                ls = slice(j * LANES, (j + 1) * LANES)
                acol2 = jnp.where(left, acum[:, ha:ha + 1], acum[:, hb:hb + 1])
                arow2 = jnp.where(left1, ac_t[ha:ha + 1, :], ac_t[hb:hb + 1, :])
                dt2 = jnp.where(left, dt_blk[:, ha:ha + 1], dt_blk[:, hb:hb + 1])
                last2 = jnp.where(left1, last[:, ha:ha + 1], last[:, hb:hb + 1])
                decay = jnp.where(causal2, jnp.exp(acol2 - arow2), 0.0)
                g2 = (cb2 * decay).astype(BF16)
                xs = tmp[rs, ls]
                xdt = xs * dt2
                x2 = jnp.where(blockdiag, jnp.concatenate([xdt, xdt], axis=0), 0.0).astype(BF16)
                sj = s_scr[j]
                y_intra = _dot(g2, x2)
                y_inter = _dot(cg_, sj.astype(BF16)) * jnp.exp(acol2)
                yc[rs, ls] = y_intra + y_inter + cdsk_ref[0:1, ls] * xs
                xw = jnp.concatenate([xdt * jnp.exp(last2 - acol2), zblk], axis=0).astype(BF16)
                s_scr[j] = sj * jnp.exp(last2) + _dot(bg_t, xw)

    yz = yc[...] * _silu(pz)
    half = GROUP_WIDTH // 2
    for g in range(2):
        yg = yz[:, g * half:(g + 1) * half]
        ms = jnp.mean(yg * yg, axis=-1, keepdims=True)
        yn = yg * lax.rsqrt(ms + NORM_EPS) * cng_ref[0:1, g * half:(g + 1) * half]
        mix[:, 2 * GROUP_WIDTH + g * half:2 * GROUP_WIDTH + (g + 1) * half] = yn.astype(BF16)

    pd = _dot(h, wd_ref[...])
    extd[HALO_CD:HALO_CD + tm, :] = pd[:, GROUP_WIDTH:2 * GROUP_WIDTH] * pd[:, 2 * GROUP_WIDTH:]
    _dwconv(extd, dcw_ref, None, tmp, D_CONV_WIDTH, HALO_CD, tm, GROUP_WIDTH)
    nd_ref[...] = extd[valid:valid + HALO_CD, :]
    extd[0:HALO_CD, :] = extd[tm:tm + HALO_CD, :]
    mix[:, 3 * GROUP_WIDTH:] = (pd[:, :GROUP_WIDTH] * tmp[:, 0:GROUP_WIDTH]).astype(BF16)

    m = _dot(mix[...], wo_ref[...])
    y_ref[...] = x + _rms(m, g3_ref[...])

    @pl.when(t == nt - 1)
    def _():
        for j in range(4):
            nh_ref[j] = s_scr[j].T


def _const_spec(shape, index):
    nd = len(shape)
    del nd
    return pl.BlockSpec(shape, lambda b, t: index, pipeline_mode=pl.Buffered(1))


def _mix(x, normg, w_in_bf, wd_bf, w_out_bf, small, states, layer, tm, valid, emit_v):
    n, seq, _ = x.shape
    nt = seq // tm
    st_b, st_c, st_h, st_d = states
    (alg, alb, aws, absT, bcw, bcb, blg, blb, ccw, ccb, cdtb, calog, cdsk, cng, dcw) = small
    l = layer

    def wspec(width, col_block):
        return pl.BlockSpec((None, D_MODEL, width), lambda b, t: (l, 0, col_block),
                            pipeline_mode=pl.Buffered(1))

    def vec(width):
        return pl.BlockSpec((None, 1, width), lambda b, t: (l, 0, 0), pipeline_mode=pl.Buffered(1))

    def mat(rows, width):
        return pl.BlockSpec((None, rows, width), lambda b, t: (l, 0, 0), pipeline_mode=pl.Buffered(1))

    in_specs = [
        pl.BlockSpec((None, tm, D_MODEL), lambda b, t: (b, t, 0)),
        pl.BlockSpec((None, 1, D_MODEL), lambda b, t: (l * 6 + 2, 0, 0), pipeline_mode=pl.Buffered(1)),
        pl.BlockSpec((None, 1, D_MODEL), lambda b, t: (l * 6 + 3, 0, 0), pipeline_mode=pl.Buffered(1)),
        wspec(1024, 0),
        wspec(1024, 1),
        wspec(512, 4),
        wspec(512, 5),
        wspec(512, 6),
        wspec(128, 28),
        pl.BlockSpec((None, D_MODEL, 3 * GROUP_WIDTH), lambda b, t: (l, 0, 0),
                     pipeline_mode=pl.Buffered(1)),
        pl.BlockSpec((None, D_MODEL, D_MODEL), lambda b, t: (l, 0, 0),
                     pipeline_mode=pl.Buffered(1)),
        vec(GROUP_WIDTH), vec(GROUP_WIDTH),
        pl.BlockSpec((None, A_HEADS, GMLP_CHUNK, GMLP_CHUNK), lambda b, t: (l, 0, 0, 0),
                     pipeline_mode=pl.Buffered(1)),
        mat(GMLP_CHUNK, A_HEADS),
        mat(B_CONV_WIDTH, GROUP_WIDTH), vec(GROUP_WIDTH), vec(GROUP_WIDTH), vec(GROUP_WIDTH),
        mat(C_CONV_WIDTH, C_XBC), vec(C_XBC), vec(LANES), vec(LANES), vec(GROUP_WIDTH), vec(GROUP_WIDTH),
        mat(D_CONV_WIDTH, GROUP_WIDTH),
        pl.BlockSpec((None, None, HALO_B, GROUP_WIDTH), lambda b, t: (l, b, 0, 0)),
        pl.BlockSpec((None, None, HALO_CD, C_XBC), lambda b, t: (l, b, 0, 0)),
        pl.BlockSpec((None, None, 4, LANES, LANES), lambda b, t: (l, b, 0, 0, 0)),
        pl.BlockSpec((None, None, HALO_CD, GROUP_WIDTH), lambda b, t: (l, b, 0, 0)),
    ]
    out_specs = [
        pl.BlockSpec((None, tm, D_MODEL), lambda b, t: (b, t, 0)),
        pl.BlockSpec((None, HALO_B, GROUP_WIDTH), lambda b, t: (b, 0, 0)),
        pl.BlockSpec((None, HALO_CD, C_XBC), lambda b, t: (b, 0, 0)),
        pl.BlockSpec((None, 4, LANES, LANES), lambda b, t: (b, 0, 0, 0)),
        pl.BlockSpec((None, HALO_CD, GROUP_WIDTH), lambda b, t: (b, 0, 0)),
    ]
    out_shape = [
        jax.ShapeDtypeStruct((n, seq, D_MODEL), F32),
        jax.ShapeDtypeStruct((n, HALO_B, GROUP_WIDTH), F32),
        jax.ShapeDtypeStruct((n, HALO_CD, C_XBC), F32),
        jax.ShapeDtypeStruct((n, 4, LANES, LANES), F32),
        jax.ShapeDtypeStruct((n, HALO_CD, GROUP_WIDTH), F32),
    ]
    if emit_v:
        out_specs.append(pl.BlockSpec((None, tm, GROUP_WIDTH), lambda b, t: (b, t, 0)))
        out_shape.append(jax.ShapeDtypeStruct((n, seq, GROUP_WIDTH), F32))
    scratch = [
        pltpu.VMEM((HALO_B + tm, GROUP_WIDTH), F32),
        pltpu.VMEM((HALO_CD + tm, C_XBC), F32),
        pltpu.VMEM((HALO_CD + tm, GROUP_WIDTH), F32),
        pltpu.VMEM((4, LANES, LANES), F32),
        pltpu.VMEM((tm, C_XBC), F32),
        pltpu.VMEM((tm, GROUP_WIDTH), F32),
        pltpu.VMEM((tm, D_MODEL), BF16),
        pltpu.VMEM((tm, LANES), F32),
        pltpu.VMEM((tm, LANES), F32),
    ]
    return pl.pallas_call(
        functools.partial(_mix_body, tm, valid, emit_v, nt),
        grid=(n, nt),
        in_specs=in_specs,
        out_specs=out_specs,
        out_shape=out_shape,
        scratch_shapes=scratch,
        compiler_params=pltpu.CompilerParams(
            dimension_semantics=("arbitrary", "arbitrary"),
            vmem_limit_bytes=VMEM_LIMIT_BYTES),
        name=f"mix_l{layer}",
    )(x, normg, normg, w_in_bf, w_in_bf, w_in_bf, w_in_bf, w_in_bf, w_in_bf, wd_bf, w_out_bf,
      alg, alb, aws, absT, bcw, bcb, blg, blb, ccw, ccb, cdtb, calog, cdsk, cng, dcw,
      st_b, st_c, st_h, st_d)


def _pad_rows_front(a, rows):
    pad = [(0, 0)] * a.ndim
    pad[-2] = (rows - a.shape[-2], 0)
    return jnp.pad(a, pad)


def _trunk(x, st_b, st_c, st_h, st_d, prep, depth, tm_ffn, tm_mix, valid, emit_v):
    (normg, ffn_in_bf, ffn_out_bf, w_in_bf, wd_bf, w_out_bf, small) = prep
    n, seq, _ = x.shape
    sb = _pad_rows_front(st_b, HALO_B)
    sc = _pad_rows_front(st_c, HALO_CD)
    sd = _pad_rows_front(st_d, HALO_CD)
    sh = st_h.reshape(st_h.shape[0], n, 4, LANES, LANES)
    new = []
    for l in range(depth):
        x2 = _ffn(x.reshape(n * seq, D_MODEL), normg, ffn_in_bf, ffn_out_bf, l, 0, tm_ffn)
        outs = _mix(x2.reshape(n, seq, D_MODEL), normg, w_in_bf, wd_bf, w_out_bf, small,
                    (sb, sc, sh, sd), l, tm_mix, valid, emit_v)
        x2 = _ffn(outs[0].reshape(n * seq, D_MODEL), normg, ffn_in_bf, ffn_out_bf, l, 1, tm_ffn)
        x = x2.reshape(n, seq, D_MODEL)
        new.append(outs[1:])
    nb = jnp.stack([s[0][:, HALO_B - (B_CONV_WIDTH - 1):] for s in new])
    nc = jnp.stack([s[1][:, HALO_CD - (C_CONV_WIDTH - 1):] for s in new])
    nh = jnp.stack([s[2].reshape(n, C_HEADS, C_HEAD_DIM, C_STATE) for s in new])
    nd = jnp.stack([s[3][:, HALO_CD - (D_CONV_WIDTH - 1):] for s in new])
    v = jnp.stack([s[4] for s in new]) if emit_v else None
    return x, v, nb, nc, nh, nd


def _prepare(norm_g, ffn_w_in, ffn_w_out, w_in, w_out, a_ln_g, a_ln_b, a_ws, a_bs,
             b_conv_w, b_conv_b, b_ln_g, b_ln_b, c_conv_w, c_conv_b, c_dt_bias, c_a_log, c_d, c_norm_g,
             d_conv_w):
    depth = norm_g.shape[0]
    normg = norm_g.reshape(depth * 6, 1, D_MODEL)
    w_in_bf = w_in.astype(BF16)
    wd_bf = w_in_bf[:, :, IN_D_START:]
    row = lambda a: a.reshape(depth, 1, a.shape[-1])
    pad_lanes = lambda a: jnp.pad(a, ((0, 0), (0, LANES - a.shape[-1]))).reshape(depth, 1, LANES)
    small = (row(a_ln_g), row(a_ln_b), a_ws, jnp.swapaxes(a_bs, 1, 2),
             b_conv_w, row(b_conv_b), row(b_ln_g), row(b_ln_b),
             c_conv_w, row(c_conv_b), pad_lanes(c_dt_bias), pad_lanes(c_a_log),
             row(jnp.repeat(c_d, C_HEAD_DIM, axis=-1)), row(c_norm_g),
             d_conv_w)
    return (normg, ffn_w_in.astype(BF16), ffn_w_out.astype(BF16), w_in_bf, wd_bf,
            w_out.astype(BF16), small)


def kernel(x_prompt, x_sample, cache_conv_b, cache_conv_c, state_ssm, cache_conv_d,
           norm_g, ffn_w_in, ffn_w_out, w_in, w_out, a_ln_g, a_ln_b, a_ws, a_bs,
           b_conv_w, b_conv_b, b_ln_g, b_ln_b, c_conv_w, c_conv_b, c_dt_bias, c_a_log, c_d, c_norm_g,
           d_conv_w):
    depth = norm_g.shape[0]
    prep = _prepare(norm_g, ffn_w_in, ffn_w_out, w_in, w_out, a_ln_g, a_ln_b, a_ws, a_bs,
                    b_conv_w, b_conv_b, b_ln_g, b_ln_b, c_conv_w, c_conv_b, c_dt_bias, c_a_log, c_d,
                    c_norm_g, d_conv_w)

    nb, seq, _ = x_prompt.shape
    zb = jnp.zeros((depth, nb, B_CONV_WIDTH - 1, GROUP_WIDTH), F32)
    zc = jnp.zeros((depth, nb, C_CONV_WIDTH - 1, C_XBC), F32)
    zh = jnp.zeros((depth, nb, C_HEADS, C_HEAD_DIM, C_STATE), F32)
    zd = jnp.zeros((depth, nb, D_CONV_WIDTH - 1, GROUP_WIDTH), F32)
    tm_mix = min(256, seq)
    tm_ffn = min(512, nb * seq)
    y_prompt, _, p_b, p_c, p_h, p_d = _trunk(x_prompt, zb, zc, zh, zd, prep, depth,
                                             tm_ffn, tm_mix, tm_mix, False)

    ns, dec, _ = x_sample.shape
    pad_to = -(-dec // SSD_BLOCK) * SSD_BLOCK
    xs = jnp.pad(x_sample, ((0, 0), (0, pad_to - dec), (0, 0)))
    y_s, s_v, s_b, s_c, s_h, s_d = _trunk(xs, cache_conv_b, cache_conv_c, state_ssm, cache_conv_d,
                                          prep, depth, min(512, ns * pad_to), pad_to, dec, True)
    return (y_prompt, y_s[:, :dec], p_b, p_c, p_h, p_d,
            s_v[:, :, :dec], s_b, s_c, s_h, s_d)
```

```python
import functools

import numpy as np
import jax
import jax.numpy as jnp
from jax import lax
from jax.experimental import pallas as pl
from jax.experimental.pallas import tpu as pltpu

F32 = jnp.float32
BF16 = jnp.bfloat16

D_MODEL = 2048
D_FF = 5632
GROUP_WIDTH = 512
A_HEADS = 4
GMLP_CHUNK = 128
B_CONV_WIDTH = 31
C_HEADS = 8
C_HEAD_DIM = 64
C_STATE = 128
C_CONV_WIDTH = 4
C_XBC = 1024
D_CONV_WIDTH = 3
SSD_BLOCK = 64
IN_D_START = 3592
NORM_EPS = 1e-6
INV_SQRT2 = float(1.0 / np.float32(np.sqrt(2.0)))

SUBLANES = 8
LANES = 128
HALO_B = 32
HALO_CD = 8
VMEM_LIMIT_BYTES = 58 * 1024 * 1024

P_A, P_B, P_Z, P_XS, P_BC, P_DT, P_D, P_END = 0, 1024, 2048, 2560, 3072, 3584, 3712, 5248

FFN_TILE_ROWS = 512
FFN_TILE_FF = 512
MIX_TILE_ROWS = 256


def _rms(x, g):
    ms = jnp.mean(x * x, axis=-1, keepdims=True)
    return x * lax.rsqrt(ms + NORM_EPS) * g


def _layer_norm(x, g, b):
    mu = jnp.mean(x, axis=-1, keepdims=True)
    xc = x - mu
    var = jnp.mean(xc * xc, axis=-1, keepdims=True)
    return xc * lax.rsqrt(var + NORM_EPS) * g + b


def _silu(x):
    return x * jax.nn.sigmoid(x)


def _dot(a, b):
    return jnp.dot(a, b, preferred_element_type=F32)


def _dot_nt(a, b):
    return lax.dot_general(a, b, (((1,), (1,)), ((), ())), preferred_element_type=F32)


def _ffn_body(nk, fused, *refs):
    if fused:
        (x_ref, mix_ref, wmix_ref, gmix_ref, gin_ref, wg_ref, wu_ref, wo_ref, gout_ref,
         o_ref, h_scr, x2_scr) = refs
    else:
        x_ref, gin_ref, wg_ref, wu_ref, wo_ref, gout_ref, o_ref, h_scr = refs
    k = pl.program_id(1)

    @pl.when(k == 0)
    def _():
        if fused:
            x2 = x_ref[...] + _rms(_dot(mix_ref[...], wmix_ref[...]), gmix_ref[...])
            x2_scr[...] = x2
        else:
            x2 = x_ref[...]
        h_scr[...] = _rms(x2, gin_ref[...]).astype(BF16)
        o_ref[...] = jnp.zeros_like(o_ref)

    h = h_scr[...]
    g = _dot(h, wg_ref[...])
    u = _dot(h, wu_ref[...])
    a = (_silu(g) * u).astype(BF16)
    o_ref[...] += _dot(a, wo_ref[...])

    @pl.when(k == nk - 1)
    def _():
        base = x2_scr[...] if fused else x_ref[...]
        o_ref[...] = base + 0.5 * _rms(o_ref[...], gout_ref[...])


def _ffn(x2d, normg, w_in_bf, w_out_bf, layer, which, mix2d=None, w_mix_bf=None):
    m = x2d.shape[0]
    tm = min(FFN_TILE_ROWS, m)
    tf = FFN_TILE_FF
    nk = D_FF // tf
    fused = mix2d is not None
    gi = layer * 6 + (0 if which == 0 else 4)
    go = gi + 1
    row = pl.BlockSpec((tm, D_MODEL), lambda i, k: (i, 0))
    gain = lambda idx: pl.BlockSpec((None, 1, D_MODEL), lambda i, k: (idx, 0, 0))
    in_specs = [row]
    args = [x2d]
    scratch = [pltpu.VMEM((tm, D_MODEL), BF16)]
    if fused:
        in_specs += [row,
                     pl.BlockSpec((None, D_MODEL, D_MODEL), lambda i, k: (layer, 0, 0),
                                  pipeline_mode=pl.Buffered(1)),
                     gain(layer * 6 + 3)]
        args += [mix2d, w_mix_bf, normg]
        scratch.append(pltpu.VMEM((tm, D_MODEL), F32))
    in_specs += [
        gain(gi),
        pl.BlockSpec((None, None, D_MODEL, tf), lambda i, k: (layer, which, 0, k)),
        pl.BlockSpec((None, None, D_MODEL, tf), lambda i, k: (layer, which, 0, k + nk)),
        pl.BlockSpec((None, None, tf, D_MODEL), lambda i, k: (layer, which, k, 0)),
        gain(go),
    ]
    args += [normg, w_in_bf, w_in_bf, w_out_bf, normg]
    return pl.pallas_call(
        functools.partial(_ffn_body, nk, fused),
        grid=(m // tm, nk),
        in_specs=in_specs,
        out_specs=pl.BlockSpec((tm, D_MODEL), lambda i, k: (i, 0)),
        out_shape=jax.ShapeDtypeStruct((m, D_MODEL), F32),
        scratch_shapes=scratch,
        compiler_params=pltpu.CompilerParams(
            dimension_semantics=("parallel", "arbitrary"),
            vmem_limit_bytes=VMEM_LIMIT_BYTES),
        name=f"ffn_l{layer}_{which}",
    )(*args)


def _dwconv(ext_ref, w_ref, b_ref, out_ref, taps, halo, tm, width):
    rows = min(128, tm)
    first = halo - (taps - 1)
    for r0 in range(0, tm, rows):
        for c0 in range(0, width, LANES):
            cs = slice(c0, c0 + LANES)
            acc = None
            for res in range(SUBLANES):
                offs = [j for j in range(first, first + taps) if j % SUBLANES == res]
                if not offs:
                    continue
                lo, hi = offs[0], offs[-1]
                if res == 0:
                    win = ext_ref[r0 + lo:r0 + hi + rows, cs]
                else:
                    span = hi - lo + rows + SUBLANES
                    base = r0 + lo - res
                    win = pltpu.roll(ext_ref[base:base + span, cs], span - res, 0)
                for j in offs:
                    term = w_ref[j - first:j - first + 1, cs] * win[j - lo:j - lo + rows]
                    acc = term if acc is None else acc + term
            if b_ref is not None:
                acc = acc + b_ref[0:1, cs]
            out_ref[r0:r0 + rows, cs] = acc


def _project(x_ref, g2_ref, w_refs, p_ref):
    h = _rms(x_ref[...], g2_ref[...]).astype(BF16)
    bounds = (P_A, P_B, P_Z, P_XS, P_BC, P_DT, P_D, P_END)
    for w_ref, lo, hi in zip(w_refs, bounds[:-1], bounds[1:]):
        p_ref[:, lo:hi] = _dot(h, w_ref[...])


def _mixers(tm, valid, p, small, scr, mix_ref, nb_ref, nc_ref, nd_ref, v_ref):
    (alg_ref, alb_ref, aws_ref, absT_ref, bcw_ref, bcb_ref, blg_ref, blb_ref,
     ccw_ref, ccb_ref, cdtb_ref, calog_ref, cdsk_ref, cng_ref, dcw_ref) = small
    extb, extc, extd, s_scr, tmp, yc, dt_scr, a_scr = scr

    pa = p[:, P_A:P_B]
    ga = pa * (lax.erf(pa * INV_SQRT2) + 1.0) * 0.5
    u = ga[:, :GROUP_WIDTH]
    vn = _layer_norm(ga[:, GROUP_WIDTH:], alg_ref[...], alb_ref[...])
    if v_ref is not None:
        v_ref[...] = vn
    vb = vn.astype(BF16)
    tc = min(GMLP_CHUNK, tm)
    rr = lax.broadcasted_iota(jnp.int32, (tc, tc), 0)
    cc = lax.broadcasted_iota(jnp.int32, (tc, tc), 1)
    for hd in range(A_HEADS):
        w = jnp.where(rr >= cc, aws_ref[hd, 0:tc, 0:tc], 0.0).astype(BF16)
        bcol = absT_ref[0:tc, hd:hd + 1]
        for c in range(tm // tc):
            rs = slice(c * tc, (c + 1) * tc)
            ls = slice(hd * LANES, (hd + 1) * LANES)
            s = _dot(w, vb[rs, ls]) + bcol
            mix_ref[rs, ls] = (u[rs, ls] * s).astype(BF16)

    extb[HALO_B:HALO_B + tm, :] = p[:, P_B:P_B + GROUP_WIDTH] * jax.nn.sigmoid(p[:, P_B + GROUP_WIDTH:P_Z])
    _dwconv(extb, bcw_ref, bcb_ref, tmp, B_CONV_WIDTH, HALO_B, tm, GROUP_WIDTH)
    nb_ref[...] = extb[valid:valid + HALO_B, :]
    extb[0:HALO_B, :] = extb[tm:tm + HALO_B, :]
    yb = _silu(_layer_norm(tmp[:, 0:GROUP_WIDTH], blg_ref[...], blb_ref[...]))
    mix_ref[:, GROUP_WIDTH:2 * GROUP_WIDTH] = yb.astype(BF16)

    extc[HALO_CD:HALO_CD + tm, :] = p[:, P_XS:P_DT]
    _dwconv(extc, ccw_ref, ccb_ref, tmp, C_CONV_WIDTH, HALO_CD, tm, C_XBC)
    nc_ref[...] = extc[valid:valid + HALO_CD, :]
    extc[0:HALO_CD, :] = extc[tm:tm + HALO_CD, :]
    tmp[...] = _silu(tmp[...])

    dpre = p[:, P_DT:P_D] + cdtb_ref[...]
    dtv = jnp.maximum(dpre, 0.0) + jnp.log1p(jnp.exp(-jnp.abs(dpre)))
    if valid < tm:
        rowid = lax.broadcasted_iota(jnp.int32, (tm, LANES), 0)
        dtv = jnp.where(rowid < valid, dtv, 0.0)
    dt_scr[...] = dtv
    a_scr[...] = dtv * (-jnp.exp(calog_ref[...]))

    bs = SSD_BLOCK
    tri = (lax.broadcasted_iota(jnp.int32, (bs, bs), 0)
           >= lax.broadcasted_iota(jnp.int32, (bs, bs), 1)).astype(F32)
    lane = lax.broadcasted_iota(jnp.int32, (bs, LANES), 1)
    lane1 = lax.broadcasted_iota(jnp.int32, (1, LANES), 1)
    lane2 = lax.broadcasted_iota(jnp.int32, (2 * bs, LANES), 1)
    row2 = lax.broadcasted_iota(jnp.int32, (2 * bs, LANES), 0)
    blockdiag = ((row2 < bs) & (lane2 < bs)) | ((row2 >= bs) & (lane2 >= bs))
    left = lane < bs
    left1 = lane1 < bs
    causal2 = lax.broadcasted_iota(jnp.int32, (bs, LANES), 0) >= jnp.where(left, lane, lane - bs)
    zblk = jnp.zeros((bs, LANES), F32)

    for c in range(tm // bs):
        rs = slice(c * bs, (c + 1) * bs)
        acum = lax.dot_general(tri, a_scr[rs, :], (((1,), (0,)), ((), ())),
                               precision=lax.Precision.HIGHEST, preferred_element_type=F32)
        dt_blk = dt_scr[rs, :]
        ac_t = jnp.concatenate([acum, acum], axis=0).T
        last = acum[bs - 1:bs, :]
        for g in range(2):
            bg_ = tmp[rs, GROUP_WIDTH + g * C_STATE:GROUP_WIDTH + (g + 1) * C_STATE]
            cg_ = tmp[rs, GROUP_WIDTH + (2 + g) * C_STATE:GROUP_WIDTH + (3 + g) * C_STATE].astype(BF16)
            cb2 = _dot_nt(cg_, jnp.concatenate([bg_, bg_], axis=0).astype(BF16))
            bg_t = jnp.concatenate([bg_, zblk], axis=0).T.astype(BF16)
            for jj in range(2):
                j = g * 2 + jj
                ha, hb = 2 * j, 2 * j + 1
                ls = slice(j * LANES, (j + 1) * LANES)
                acol2 = jnp.where(left, acum[:, ha:ha + 1], acum[:, hb:hb + 1])
                arow2 = jnp.where(left1, ac_t[ha:ha + 1, :], ac_t[hb:hb + 1, :])
                dt2 = jnp.where(left, dt_blk[:, ha:ha + 1], dt_blk[:, hb:hb + 1])
                last2 = jnp.where(left1, last[:, ha:ha + 1], last[:, hb:hb + 1])
                decay = jnp.where(causal2, jnp.exp(acol2 - arow2), 0.0)
                g2 = (cb2 * decay).astype(BF16)
                xs = tmp[rs, ls]
                xdt = xs * dt2
                x2 = jnp.where(blockdiag, jnp.concatenate([xdt, xdt], axis=0), 0.0).astype(BF16)
                sj = s_scr[j]
                y_intra = _dot(g2, x2)
                y_inter = _dot(cg_, sj.astype(BF16)) * jnp.exp(acol2)
                yc[rs, ls] = y_intra + y_inter + cdsk_ref[0:1, ls] * xs
                xw = jnp.concatenate([xdt * jnp.exp(last2 - acol2), zblk], axis=0).astype(BF16)
                s_scr[j] = sj * jnp.exp(last2) + _dot(bg_t, xw)

    yz = yc[...] * _silu(p[:, P_Z:P_XS])
    half = GROUP_WIDTH // 2
    for g in range(2):
        yg = yz[:, g * half:(g + 1) * half]
        ms = jnp.mean(yg * yg, axis=-1, keepdims=True)
        yn = yg * lax.rsqrt(ms + NORM_EPS) * cng_ref[0:1, g * half:(g + 1) * half]
        mix_ref[:, 2 * GROUP_WIDTH + g * half:2 * GROUP_WIDTH + (g + 1) * half] = yn.astype(BF16)

    extd[HALO_CD:HALO_CD + tm, :] = (p[:, P_D + GROUP_WIDTH:P_D + 2 * GROUP_WIDTH]
                                     * p[:, P_D + 2 * GROUP_WIDTH:P_END])
    _dwconv(extd, dcw_ref, None, tmp, D_CONV_WIDTH, HALO_CD, tm, GROUP_WIDTH)
    nd_ref[...] = extd[valid:valid + HALO_CD, :]
    extd[0:HALO_CD, :] = extd[tm:tm + HALO_CD, :]
    mix_ref[:, 3 * GROUP_WIDTH:] = (p[:, P_D:P_D + GROUP_WIDTH] * tmp[:, 0:GROUP_WIDTH]).astype(BF16)


def _mix_body(tm, nt, valid, emit_v, *refs):
    n_in = 28
    x_ref = refs[0]
    g2_ref = refs[1]
    w_refs = refs[2:9]
    small = refs[9:24]
    stb_ref, stc_ref, sth_ref, std_ref = refs[24:28]
    n_out = 6 if emit_v else 5
    outs = refs[n_in:n_in + n_out]
    mix_ref, nb_ref, nc_ref, nh_ref, nd_ref = outs[:5]
    v_ref = outs[5] if emit_v else None
    scr = refs[n_in + n_out:]
    p_scr = scr[0]
    extb, extc, extd, s_scr = scr[1:5]
    work = scr[1:]
    t = pl.program_id(1)

    @pl.when(t == 0)
    def _():
        extb[0:HALO_B, :] = stb_ref[...]
        extc[0:HALO_CD, :] = stc_ref[...]
        extd[0:HALO_CD, :] = std_ref[...]
        for j in range(4):
            s_scr[j] = sth_ref[j].T

    _project(x_ref, g2_ref, w_refs, p_scr)
    _mixers(tm, valid, p_scr, small, work, mix_ref, nb_ref, nc_ref, nd_ref, v_ref)

    @pl.when(t == nt - 1)
    def _():
        for j in range(4):
            nh_ref[j] = s_scr[j].T


def _mix(x, normg, w_in_bf, wd_bf, small, states, layer, valid, emit_v):
    n, seq, _ = x.shape
    tm = min(MIX_TILE_ROWS, seq)
    assert seq % tm == 0
    nt = seq // tm
    st_b, st_c, st_h, st_d = states
    l = layer
    once = pl.Buffered(1)

    def wspec(width, col_block):
        return pl.BlockSpec((None, D_MODEL, width), lambda b, s: (l, 0, col_block), pipeline_mode=once)

    def vec(width):
        return pl.BlockSpec((None, 1, width), lambda b, s: (l, 0, 0), pipeline_mode=once)

    def mat(nrows, width):
        return pl.BlockSpec((None, nrows, width), lambda b, s: (l, 0, 0), pipeline_mode=once)

    in_specs = [pl.BlockSpec((None, tm, D_MODEL), lambda b, s: (b, s, 0))]
    args = [x]
    in_specs += [
        pl.BlockSpec((None, 1, D_MODEL), lambda b, s: (l * 6 + 2, 0, 0), pipeline_mode=once),
        wspec(1024, 0),
        wspec(1024, 1),
        wspec(512, 4),
        wspec(512, 5),
        wspec(512, 6),
        wspec(128, 28),
        pl.BlockSpec((None, D_MODEL, 3 * GROUP_WIDTH), lambda b, s: (l, 0, 0), pipeline_mode=once),
        vec(GROUP_WIDTH), vec(GROUP_WIDTH),
        pl.BlockSpec((None, A_HEADS, GMLP_CHUNK, GMLP_CHUNK), lambda b, s: (l, 0, 0, 0),
                     pipeline_mode=once),
        mat(GMLP_CHUNK, A_HEADS),
        mat(B_CONV_WIDTH, GROUP_WIDTH), vec(GROUP_WIDTH), vec(GROUP_WIDTH), vec(GROUP_WIDTH),
        mat(C_CONV_WIDTH, C_XBC), vec(C_XBC), vec(LANES), vec(LANES), vec(GROUP_WIDTH), vec(GROUP_WIDTH),
        mat(D_CONV_WIDTH, GROUP_WIDTH),
        pl.BlockSpec((None, None, HALO_B, GROUP_WIDTH), lambda b, s: (l, b, 0, 0)),
        pl.BlockSpec((None, None, HALO_CD, C_XBC), lambda b, s: (l, b, 0, 0)),
        pl.BlockSpec((None, None, 4, LANES, LANES), lambda b, s: (l, b, 0, 0, 0)),
        pl.BlockSpec((None, None, HALO_CD, GROUP_WIDTH), lambda b, s: (l, b, 0, 0)),
    ]
    args += [normg, w_in_bf, w_in_bf, w_in_bf, w_in_bf, w_in_bf, w_in_bf, wd_bf, *small,
             st_b, st_c, st_h, st_d]
    out_specs = [
        pl.BlockSpec((None, tm, D_MODEL), lambda b, s: (b, s, 0)),
        pl.BlockSpec((None, HALO_B, GROUP_WIDTH), lambda b, s: (b, 0, 0)),
        pl.BlockSpec((None, HALO_CD, C_XBC), lambda b, s: (b, 0, 0)),
        pl.BlockSpec((None, 4, LANES, LANES), lambda b, s: (b, 0, 0, 0)),
        pl.BlockSpec((None, HALO_CD, GROUP_WIDTH), lambda b, s: (b, 0, 0)),
    ]
    out_shape = [
        jax.ShapeDtypeStruct((n, seq, D_MODEL), BF16),
        jax.ShapeDtypeStruct((n, HALO_B, GROUP_WIDTH), F32),
        jax.ShapeDtypeStruct((n, HALO_CD, C_XBC), F32),
        jax.ShapeDtypeStruct((n, 4, LANES, LANES), F32),
        jax.ShapeDtypeStruct((n, HALO_CD, GROUP_WIDTH), F32),
    ]
    if emit_v:
        out_specs.append(pl.BlockSpec((None, tm, GROUP_WIDTH), lambda b, s: (b, s, 0)))
        out_shape.append(jax.ShapeDtypeStruct((n, seq, GROUP_WIDTH), F32))
    scratch = [
        pltpu.VMEM((tm, P_END), F32),
        pltpu.VMEM((HALO_B + tm, GROUP_WIDTH), F32),
        pltpu.VMEM((HALO_CD + tm, C_XBC), F32),
        pltpu.VMEM((HALO_CD + tm, GROUP_WIDTH), F32),
        pltpu.VMEM((4, LANES, LANES), F32),
        pltpu.VMEM((tm, C_XBC), F32),
        pltpu.VMEM((tm, GROUP_WIDTH), F32),
        pltpu.VMEM((tm, LANES), F32),
        pltpu.VMEM((tm, LANES), F32),
    ]
    return pl.pallas_call(
        functools.partial(_mix_body, tm, nt, valid, emit_v),
        grid=(n, nt),
        in_specs=in_specs,
        out_specs=out_specs,
        out_shape=out_shape,
        scratch_shapes=scratch,
        compiler_params=pltpu.CompilerParams(
            dimension_semantics=("arbitrary", "arbitrary"),
            vmem_limit_bytes=VMEM_LIMIT_BYTES),
        name=f"mix_l{layer}",
    )(*args)


def _pad_rows_front(a, rows):
    pad = [(0, 0)] * a.ndim
    pad[-2] = (rows - a.shape[-2], 0)
    return jnp.pad(a, pad)


def _trunk(x, st_b, st_c, st_h, st_d, prep, depth, emit_v):
    (normg, ffn_in_bf, ffn_out_bf, w_in_bf, wd_bf, w_out_bf, small) = prep
    n, seq, _ = x.shape
    seq_pad = -(-seq // SSD_BLOCK) * SSD_BLOCK
    sb = _pad_rows_front(st_b, HALO_B)
    sc = _pad_rows_front(st_c, HALO_CD)
    sd = _pad_rows_front(st_d, HALO_CD)
    sh = st_h.reshape(st_h.shape[0], n, 4, LANES, LANES)
    x2d = x.reshape(n * seq, D_MODEL)
    new = []
    for l in range(depth):
        x2d = _ffn(x2d, normg, ffn_in_bf, ffn_out_bf, l, 0)
        xm = x2d.reshape(n, seq, D_MODEL)
        if seq_pad != seq:
            xm = jnp.pad(xm, ((0, 0), (0, seq_pad - seq), (0, 0)))
        outs = _mix(xm, normg, w_in_bf, wd_bf, small, (sb, sc, sh, sd), l,
                    min(seq, MIX_TILE_ROWS), emit_v)
        mixed = outs[0][:, :seq].reshape(n * seq, D_MODEL)
        x2d = _ffn(x2d, normg, ffn_in_bf, ffn_out_bf, l, 1, mixed, w_out_bf)
        new.append(outs[1:])
    nb = jnp.stack([s[0][:, HALO_B - (B_CONV_WIDTH - 1):] for s in new])
    nc = jnp.stack([s[1][:, HALO_CD - (C_CONV_WIDTH - 1):] for s in new])
    nh = jnp.stack([s[2].reshape(n, C_HEADS, C_HEAD_DIM, C_STATE) for s in new])
    nd = jnp.stack([s[3][:, HALO_CD - (D_CONV_WIDTH - 1):] for s in new])
    v = jnp.stack([s[4][:, :seq] for s in new]) if emit_v else None
    return x2d.reshape(n, seq, D_MODEL), v, nb, nc, nh, nd


def _prepare(norm_g, ffn_w_in, ffn_w_out, w_in, w_out, a_ln_g, a_ln_b, a_ws, a_bs,
             b_conv_w, b_conv_b, b_ln_g, b_ln_b, c_conv_w, c_conv_b, c_dt_bias, c_a_log, c_d, c_norm_g,
             d_conv_w):
    depth = norm_g.shape[0]
    normg = norm_g.reshape(depth * 6, 1, D_MODEL)
    w_in_bf = w_in.astype(BF16)
    wd_bf = w_in_bf[:, :, IN_D_START:]
    row = lambda a: a.reshape(depth, 1, a.shape[-1])
    pad_lanes = lambda a: jnp.pad(a, ((0, 0), (0, LANES - a.shape[-1]))).reshape(depth, 1, LANES)
    small = (row(a_ln_g), row(a_ln_b), a_ws, jnp.swapaxes(a_bs, 1, 2),
             b_conv_w, row(b_conv_b), row(b_ln_g), row(b_ln_b),
             c_conv_w, row(c_conv_b), pad_lanes(c_dt_bias), pad_lanes(c_a_log),
             row(jnp.repeat(c_d, C_HEAD_DIM, axis=-1)), row(c_norm_g),
             d_conv_w)
    return (normg, ffn_w_in.astype(BF16), ffn_w_out.astype(BF16), w_in_bf, wd_bf,
            w_out.astype(BF16), small)


def kernel(x_prompt, x_sample, cache_conv_b, cache_conv_c, state_ssm, cache_conv_d,
           norm_g, ffn_w_in, ffn_w_out, w_in, w_out, a_ln_g, a_ln_b, a_ws, a_bs,
           b_conv_w, b_conv_b, b_ln_g, b_ln_b, c_conv_w, c_conv_b, c_dt_bias, c_a_log, c_d, c_norm_g,
           d_conv_w):
    depth = norm_g.shape[0]
    prep = _prepare(norm_g, ffn_w_in, ffn_w_out, w_in, w_out, a_ln_g, a_ln_b, a_ws, a_bs,
                    b_conv_w, b_conv_b, b_ln_g, b_ln_b, c_conv_w, c_conv_b, c_dt_bias, c_a_log, c_d,
                    c_norm_g, d_conv_w)

    nb = x_prompt.shape[0]
    zb = jnp.zeros((depth, nb, B_CONV_WIDTH - 1, GROUP_WIDTH), F32)
    zc = jnp.zeros((depth, nb, C_CONV_WIDTH - 1, C_XBC), F32)
    zh = jnp.zeros((depth, nb, C_HEADS, C_HEAD_DIM, C_STATE), F32)
    zd = jnp.zeros((depth, nb, D_CONV_WIDTH - 1, GROUP_WIDTH), F32)
    y_prompt, _, p_b, p_c, p_h, p_d = _trunk(x_prompt, zb, zc, zh, zd, prep, depth, False)

    y_s, s_v, s_b, s_c, s_h, s_d = _trunk(x_sample, cache_conv_b, cache_conv_c, state_ssm,
                                          cache_conv_d, prep, depth, True)
    return (y_prompt, y_s, p_b, p_c, p_h, p_d, s_v, s_b, s_c, s_h, s_d)
```

```python
import functools

import numpy as np
import jax
import jax.numpy as jnp
from jax import lax
from jax.experimental import pallas as pl
from jax.experimental.pallas import tpu as pltpu

F32 = jnp.float32
BF16 = jnp.bfloat16

D_MODEL = 2048
D_FF = 5632
GROUP_WIDTH = 512
A_HEADS = 4
GMLP_CHUNK = 128
B_CONV_WIDTH = 31
C_HEADS = 8
C_HEAD_DIM = 64
C_STATE = 128
C_CONV_WIDTH = 4
C_XBC = 1024
D_CONV_WIDTH = 3
SSD_BLOCK = 64
IN_D_START = 3592
NORM_EPS = 1e-6
INV_SQRT2 = float(1.0 / np.float32(np.sqrt(2.0)))

SUBLANES = 8
LANES = 128
HALO_B = 32
HALO_CD = 8
VMEM_LIMIT_BYTES = 58 * 1024 * 1024

P_A, P_B, P_Z, P_XS, P_BC, P_DT, P_D, P_END = 0, 1024, 2048, 2560, 3072, 3584, 3712, 5248

FFN_TILE_ROWS = 512
FFN_TILE_FF = 512
FFN_CAST_TILE_FF = 256
MIX_TILE_ROWS = 256


def _rms(x, g):
    ms = jnp.mean(x * x, axis=-1, keepdims=True)
    return x * lax.rsqrt(ms + NORM_EPS) * g


def _layer_norm(x, g, b):
    mu = jnp.mean(x, axis=-1, keepdims=True)
    xc = x - mu
    var = jnp.mean(xc * xc, axis=-1, keepdims=True)
    return xc * lax.rsqrt(var + NORM_EPS) * g + b


def _silu(x):
    return x * jax.nn.sigmoid(x)


def _dot(a, b):
    return jnp.dot(a, b, preferred_element_type=F32)


def _dot_nt(a, b):
    return lax.dot_general(a, b, (((1,), (1,)), ((), ())), preferred_element_type=F32)


def _ffn_body(nk, fused, cast, *refs):
    refs = list(refs)
    x_ref = refs.pop(0)
    if fused:
        mix_ref, wmix_ref, gmix_ref = refs[:3]
        del refs[:3]
    gin_ref, wg_ref, wu_ref, wo_ref, gout_ref, o_ref = refs[:6]
    del refs[:6]
    if cast:
        wg_bf_ref, wu_bf_ref, wo_bf_ref = refs[:3]
        del refs[:3]
    h_scr = refs.pop(0)
    x2_scr = refs.pop(0) if fused else None
    k = pl.program_id(1)

    @pl.when(k == 0)
    def _():
        if fused:
            x2 = x_ref[...] + _rms(_dot(mix_ref[...], wmix_ref[...]), gmix_ref[...])
            x2_scr[...] = x2
        else:
            x2 = x_ref[...]
        h_scr[...] = _rms(x2, gin_ref[...]).astype(BF16)
        o_ref[...] = jnp.zeros_like(o_ref)

    wg, wu, wo = wg_ref[...], wu_ref[...], wo_ref[...]
    if cast:
        wg, wu, wo = wg.astype(BF16), wu.astype(BF16), wo.astype(BF16)
        wg_bf_ref[...] = wg
        wu_bf_ref[...] = wu
        wo_bf_ref[...] = wo
    h = h_scr[...]
    g = _dot(h, wg)
    u = _dot(h, wu)
    a = (_silu(g) * u).astype(BF16)
    o_ref[...] += _dot(a, wo)

    @pl.when(k == nk - 1)
    def _():
        base = x2_scr[...] if fused else x_ref[...]
        o_ref[...] = base + _rms(o_ref[...], 0.5 * gout_ref[...])


def _ffn(x2d, normg, weights, layer, which, mix2d=None, w_mix_bf=None):
    m = x2d.shape[0]
    fused = mix2d is not None
    cast = len(weights) == 2
    tm = min(FFN_TILE_ROWS, m)
    tf = FFN_CAST_TILE_FF if cast else FFN_TILE_FF
    nk = D_FF // tf
    gi = layer * 6 + (0 if which == 0 else 4)
    go = gi + 1
    row = pl.BlockSpec((tm, D_MODEL), lambda i, k: (i, 0))
    gain = lambda idx: pl.BlockSpec((None, 1, D_MODEL), lambda i, k: (idx, 0, 0))
    col_blk = pl.BlockSpec((D_MODEL, tf), lambda i, k: (0, k))
    row_blk = pl.BlockSpec((tf, D_MODEL), lambda i, k: (k, 0))
    in_specs = [row]
    args = [x2d]
    scratch = [pltpu.VMEM((tm, D_MODEL), BF16)]
    if fused:
        in_specs += [row,
                     pl.BlockSpec((None, D_MODEL, D_MODEL), lambda i, k: (layer, 0, 0),
                                  pipeline_mode=pl.Buffered(1)),
                     gain(layer * 6 + 3)]
        args += [mix2d, w_mix_bf, normg]
        scratch.append(pltpu.VMEM((tm, D_MODEL), F32))
    in_specs.append(gain(gi))
    args.append(normg)
    out_specs = [row]
    out_shape = [jax.ShapeDtypeStruct((m, D_MODEL), F32)]
    if cast:
        w_in, w_out = weights
        in_specs += [
            pl.BlockSpec((None, None, D_MODEL, tf), lambda i, k: (layer, which, 0, k)),
            pl.BlockSpec((None, None, D_MODEL, tf), lambda i, k: (layer, which, 0, k + nk)),
            pl.BlockSpec((None, None, tf, D_MODEL), lambda i, k: (layer, which, k, 0)),
        ]
        args += [w_in, w_in, w_out]
        out_specs += [col_blk, col_blk, row_blk]
        out_shape += [jax.ShapeDtypeStruct((D_MODEL, D_FF), BF16),
                      jax.ShapeDtypeStruct((D_MODEL, D_FF), BF16),
                      jax.ShapeDtypeStruct((D_FF, D_MODEL), BF16)]
    else:
        in_specs += [col_blk, col_blk, row_blk]
        args += list(weights)
    in_specs.append(gain(go))
    args.append(normg)
    outs = pl.pallas_call(
        functools.partial(_ffn_body, nk, fused, cast),
        grid=(m // tm, nk),
        in_specs=in_specs,
        out_specs=out_specs,
        out_shape=out_shape,
        scratch_shapes=scratch,
        compiler_params=pltpu.CompilerParams(
            dimension_semantics=("arbitrary" if cast else "parallel", "arbitrary"),
            vmem_limit_bytes=VMEM_LIMIT_BYTES),
        name=f"ffn_l{layer}_{which}",
    )(*args)
    return (outs[0], tuple(outs[1:])) if cast else outs[0]


def _dwconv(ext_ref, w_ref, b_ref, out_ref, taps, halo, tm, width):
    rows = min(128, tm)
    first = halo - (taps - 1)
    for r0 in range(0, tm, rows):
        for c0 in range(0, width, LANES):
            cs = slice(c0, c0 + LANES)
            acc = None
            for res in range(SUBLANES):
                offs = [j for j in range(first, first + taps) if j % SUBLANES == res]
                if not offs:
                    continue
                lo, hi = offs[0], offs[-1]
                if res == 0:
                    win = ext_ref[r0 + lo:r0 + hi + rows, cs]
                else:
                    span = hi - lo + rows + SUBLANES
                    base = r0 + lo - res
                    win = pltpu.roll(ext_ref[base:base + span, cs], span - res, 0)
                for j in offs:
                    term = w_ref[j - first:j - first + 1, cs] * win[j - lo:j - lo + rows]
                    acc = term if acc is None else acc + term
            if b_ref is not None:
                acc = acc + b_ref[0:1, cs]
            out_ref[r0:r0 + rows, cs] = acc


def _project(x_ref, g2_ref, w_refs, p_ref):
    h = _rms(x_ref[...], g2_ref[...]).astype(BF16)
    bounds = (P_A, P_B, P_Z, P_XS, P_BC, P_DT, P_D, P_END)
    for w_ref, lo, hi in zip(w_refs, bounds[:-1], bounds[1:]):
        p_ref[:, lo:hi] = _dot(h, w_ref[...])


def _mixers(tm, valid, p, small, scr, mix_ref, nb_ref, nc_ref, nd_ref, v_ref):
    (alg_ref, alb_ref, aws_ref, absT_ref, bcw_ref, bcb_ref, blg_ref, blb_ref,
     ccw_ref, ccb_ref, cdtb_ref, calog_ref, cdsk_ref, cng_ref, dcw_ref) = small
    extb, extc, extd, s_scr, tmp, yc, dt_scr, a_scr = scr

    pa = p[:, P_A:P_B]
    ga = pa * (lax.erf(pa * INV_SQRT2) + 1.0) * 0.5
    u = ga[:, :GROUP_WIDTH]
    vn = _layer_norm(ga[:, GROUP_WIDTH:], alg_ref[...], alb_ref[...])
    if v_ref is not None:
        v_ref[...] = vn
    vb = vn.astype(BF16)
    tc = min(GMLP_CHUNK, tm)
    rr = lax.broadcasted_iota(jnp.int32, (tc, tc), 0)
    cc = lax.broadcasted_iota(jnp.int32, (tc, tc), 1)
    for hd in range(A_HEADS):
        w = jnp.where(rr >= cc, aws_ref[hd, 0:tc, 0:tc], 0.0).astype(BF16)
        bcol = absT_ref[0:tc, hd:hd + 1]
        for c in range(tm // tc):
            rs = slice(c * tc, (c + 1) * tc)
            ls = slice(hd * LANES, (hd + 1) * LANES)
            s = _dot(w, vb[rs, ls]) + bcol
            mix_ref[rs, ls] = (u[rs, ls] * s).astype(BF16)

    extb[HALO_B:HALO_B + tm, :] = p[:, P_B:P_B + GROUP_WIDTH] * jax.nn.sigmoid(p[:, P_B + GROUP_WIDTH:P_Z])
    _dwconv(extb, bcw_ref, bcb_ref, tmp, B_CONV_WIDTH, HALO_B, tm, GROUP_WIDTH)
    nb_ref[...] = extb[valid:valid + HALO_B, :]
    extb[0:HALO_B, :] = extb[tm:tm + HALO_B, :]
    yb = _silu(_layer_norm(tmp[:, 0:GROUP_WIDTH], blg_ref[...], blb_ref[...]))
    mix_ref[:, GROUP_WIDTH:2 * GROUP_WIDTH] = yb.astype(BF16)

    extc[HALO_CD:HALO_CD + tm, :] = p[:, P_XS:P_DT]
    _dwconv(extc, ccw_ref, ccb_ref, tmp, C_CONV_WIDTH, HALO_CD, tm, C_XBC)
    nc_ref[...] = extc[valid:valid + HALO_CD, :]
    extc[0:HALO_CD, :] = extc[tm:tm + HALO_CD, :]
    tmp[...] = _silu(tmp[...])

    dpre = p[:, P_DT:P_D] + cdtb_ref[...]
    dtv = jnp.maximum(dpre, 0.0) + jnp.log1p(jnp.exp(-jnp.abs(dpre)))
    if valid < tm:
        rowid = lax.broadcasted_iota(jnp.int32, (tm, LANES), 0)
        dtv = jnp.where(rowid < valid, dtv, 0.0)
    dt_scr[...] = dtv
    a_scr[...] = dtv * (-jnp.exp(calog_ref[...]))

    bs = SSD_BLOCK
    tri = (lax.broadcasted_iota(jnp.int32, (bs, bs), 0)
           >= lax.broadcasted_iota(jnp.int32, (bs, bs), 1)).astype(F32)
    lane = lax.broadcasted_iota(jnp.int32, (bs, LANES), 1)
    lane1 = lax.broadcasted_iota(jnp.int32, (1, LANES), 1)
    lane2 = lax.broadcasted_iota(jnp.int32, (2 * bs, LANES), 1)
    row2 = lax.broadcasted_iota(jnp.int32, (2 * bs, LANES), 0)
    blockdiag = ((row2 < bs) & (lane2 < bs)) | ((row2 >= bs) & (lane2 >= bs))
    left = lane < bs
    left1 = lane1 < bs
    causal2 = lax.broadcasted_iota(jnp.int32, (bs, LANES), 0) >= jnp.where(left, lane, lane - bs)
    zblk = jnp.zeros((bs, LANES), F32)

    for c in range(tm // bs):
        rs = slice(c * bs, (c + 1) * bs)
        acum = lax.dot_general(tri, a_scr[rs, :], (((1,), (0,)), ((), ())),
                               precision=lax.Precision.HIGHEST, preferred_element_type=F32)
        dt_blk = dt_scr[rs, :]
        ac_t = jnp.concatenate([acum, acum], axis=0).T
        last = acum[bs - 1:bs, :]
        for g in range(2):
            bg_ = tmp[rs, GROUP_WIDTH + g * C_STATE:GROUP_WIDTH + (g + 1) * C_STATE]
            cg_ = tmp[rs, GROUP_WIDTH + (2 + g) * C_STATE:GROUP_WIDTH + (3 + g) * C_STATE].astype(BF16)
            cb2 = _dot_nt(cg_, jnp.concatenate([bg_, bg_], axis=0).astype(BF16))
            bg_t = jnp.concatenate([bg_, zblk], axis=0).T.astype(BF16)
            for jj in range(2):
                j = g * 2 + jj
                ha, hb = 2 * j, 2 * j + 1
                ls = slice(j * LANES, (j + 1) * LANES)
                acol2 = jnp.where(left, acum[:, ha:ha + 1], acum[:, hb:hb + 1])
                arow2 = jnp.where(left1, ac_t[ha:ha + 1, :], ac_t[hb:hb + 1, :])
                dt2 = jnp.where(left, dt_blk[:, ha:ha + 1], dt_blk[:, hb:hb + 1])
                last2 = jnp.where(left1, last[:, ha:ha + 1], last[:, hb:hb + 1])
                decay = jnp.where(causal2, jnp.exp(acol2 - arow2), 0.0)
                g2 = (cb2 * decay).astype(BF16)
                xs = tmp[rs, ls]
                xdt = xs * dt2
                x2 = jnp.where(blockdiag, jnp.concatenate([xdt, xdt], axis=0), 0.0).astype(BF16)
                sj = s_scr[j]
                y_intra = _dot(g2, x2)
                y_inter = _dot(cg_, sj.astype(BF16)) * jnp.exp(acol2)
                yc[rs, ls] = y_intra + y_inter + cdsk_ref[0:1, ls] * xs
                xw = jnp.concatenate([xdt * jnp.exp(last2 - acol2), zblk], axis=0).astype(BF16)
                s_scr[j] = sj * jnp.exp(last2) + _dot(bg_t, xw)

    yz = yc[...] * _silu(p[:, P_Z:P_XS])
    half = GROUP_WIDTH // 2
    for g in range(2):
        yg = yz[:, g * half:(g + 1) * half]
        ms = jnp.mean(yg * yg, axis=-1, keepdims=True)
        yn = yg * lax.rsqrt(ms + NORM_EPS) * cng_ref[0:1, g * half:(g + 1) * half]
        mix_ref[:, 2 * GROUP_WIDTH + g * half:2 * GROUP_WIDTH + (g + 1) * half] = yn.astype(BF16)

    extd[HALO_CD:HALO_CD + tm, :] = (p[:, P_D + GROUP_WIDTH:P_D + 2 * GROUP_WIDTH]
                                     * p[:, P_D + 2 * GROUP_WIDTH:P_END])
    _dwconv(extd, dcw_ref, None, tmp, D_CONV_WIDTH, HALO_CD, tm, GROUP_WIDTH)
    nd_ref[...] = extd[valid:valid + HALO_CD, :]
    extd[0:HALO_CD, :] = extd[tm:tm + HALO_CD, :]
    mix_ref[:, 3 * GROUP_WIDTH:] = (p[:, P_D:P_D + GROUP_WIDTH] * tmp[:, 0:GROUP_WIDTH]).astype(BF16)


def _mix_body(tm, nt, valid, emit_v, *refs):
    n_in = 28
    x_ref = refs[0]
    g2_ref = refs[1]
    w_refs = refs[2:9]
    small = refs[9:24]
    stb_ref, stc_ref, sth_ref, std_ref = refs[24:28]
    n_out = 6 if emit_v else 5
    outs = refs[n_in:n_in + n_out]
    mix_ref, nb_ref, nc_ref, nh_ref, nd_ref = outs[:5]
    v_ref = outs[5] if emit_v else None
    scr = refs[n_in + n_out:]
    p_scr = scr[0]
    extb, extc, extd, s_scr = scr[1:5]
    work = scr[1:]
    t = pl.program_id(1)

    @pl.when(t == 0)
    def _():
        extb[0:HALO_B, :] = stb_ref[...]
        extc[0:HALO_CD, :] = stc_ref[...]
        extd[0:HALO_CD, :] = std_ref[...]
        for j in range(4):
            s_scr[j] = sth_ref[j].T

    _project(x_ref, g2_ref, w_refs, p_scr)
    _mixers(tm, valid, p_scr, small, work, mix_ref, nb_ref, nc_ref, nd_ref, v_ref)

    @pl.when(t == nt - 1)
    def _():
        for j in range(4):
            nh_ref[j] = s_scr[j].T


def _mix(x, normg, w_in_bf, wd_bf, small, states, layer, valid, emit_v):
    n, seq, _ = x.shape
    tm = min(MIX_TILE_ROWS, seq)
    assert seq % tm == 0
    nt = seq // tm
    st_b, st_c, st_h, st_d = states
    l = layer
    once = pl.Buffered(1)

    def wspec(width, col_block):
        return pl.BlockSpec((None, D_MODEL, width), lambda b, s: (l, 0, col_block), pipeline_mode=once)

    def vec(width):
        return pl.BlockSpec((None, 1, width), lambda b, s: (l, 0, 0), pipeline_mode=once)

    def mat(nrows, width):
        return pl.BlockSpec((None, nrows, width), lambda b, s: (l, 0, 0), pipeline_mode=once)

    in_specs = [pl.BlockSpec((None, tm, D_MODEL), lambda b, s: (b, s, 0))]
    args = [x]
    in_specs += [
        pl.BlockSpec((None, 1, D_MODEL), lambda b, s: (l * 6 + 2, 0, 0), pipeline_mode=once),
        wspec(1024, 0),
        wspec(1024, 1),
        wspec(512, 4),
        wspec(512, 5),
        wspec(512, 6),
        wspec(128, 28),
        pl.BlockSpec((None, D_MODEL, 3 * GROUP_WIDTH), lambda b, s: (l, 0, 0), pipeline_mode=once),
        vec(GROUP_WIDTH), vec(GROUP_WIDTH),
        pl.BlockSpec((None, A_HEADS, GMLP_CHUNK, GMLP_CHUNK), lambda b, s: (l, 0, 0, 0),
                     pipeline_mode=once),
        mat(GMLP_CHUNK, A_HEADS),
        mat(B_CONV_WIDTH, GROUP_WIDTH), vec(GROUP_WIDTH), vec(GROUP_WIDTH), vec(GROUP_WIDTH),
        mat(C_CONV_WIDTH, C_XBC), vec(C_XBC), vec(LANES), vec(LANES), vec(GROUP_WIDTH), vec(GROUP_WIDTH),
        mat(D_CONV_WIDTH, GROUP_WIDTH),
        pl.BlockSpec((None, None, HALO_B, GROUP_WIDTH), lambda b, s: (l, b, 0, 0)),
        pl.BlockSpec((None, None, HALO_CD, C_XBC), lambda b, s: (l, b, 0, 0)),
        pl.BlockSpec((None, None, 4, LANES, LANES), lambda b, s: (l, b, 0, 0, 0)),
        pl.BlockSpec((None, None, HALO_CD, GROUP_WIDTH), lambda b, s: (l, b, 0, 0)),
    ]
    args += [normg, w_in_bf, w_in_bf, w_in_bf, w_in_bf, w_in_bf, w_in_bf, wd_bf, *small,
             st_b, st_c, st_h, st_d]
    out_specs = [
        pl.BlockSpec((None, tm, D_MODEL), lambda b, s: (b, s, 0)),
        pl.BlockSpec((None, HALO_B, GROUP_WIDTH), lambda b, s: (b, 0, 0)),
        pl.BlockSpec((None, HALO_CD, C_XBC), lambda b, s: (b, 0, 0)),
        pl.BlockSpec((None, 4, LANES, LANES), lambda b, s: (b, 0, 0, 0)),
        pl.BlockSpec((None, HALO_CD, GROUP_WIDTH), lambda b, s: (b, 0, 0)),
    ]
    out_shape = [
        jax.ShapeDtypeStruct((n, seq, D_MODEL), BF16),
        jax.ShapeDtypeStruct((n, HALO_B, GROUP_WIDTH), F32),
        jax.ShapeDtypeStruct((n, HALO_CD, C_XBC), F32),
        jax.ShapeDtypeStruct((n, 4, LANES, LANES), F32),
        jax.ShapeDtypeStruct((n, HALO_CD, GROUP_WIDTH), F32),
    ]
    if emit_v:
        out_specs.append(pl.BlockSpec((None, tm, GROUP_WIDTH), lambda b, s: (b, s, 0)))
        out_shape.append(jax.ShapeDtypeStruct((n, seq, GROUP_WIDTH), F32))
    scratch = [
        pltpu.VMEM((tm, P_END), F32),
        pltpu.VMEM((HALO_B + tm, GROUP_WIDTH), F32),
        pltpu.VMEM((HALO_CD + tm, C_XBC), F32),
        pltpu.VMEM((HALO_CD + tm, GROUP_WIDTH), F32),
        pltpu.VMEM((4, LANES, LANES), F32),
        pltpu.VMEM((tm, C_XBC), F32),
        pltpu.VMEM((tm, GROUP_WIDTH), F32),
        pltpu.VMEM((tm, LANES), F32),
        pltpu.VMEM((tm, LANES), F32),
    ]
    return pl.pallas_call(
        functools.partial(_mix_body, tm, nt, valid, emit_v),
        grid=(n, nt),
        in_specs=in_specs,
        out_specs=out_specs,
        out_shape=out_shape,
        scratch_shapes=scratch,
        compiler_params=pltpu.CompilerParams(
            dimension_semantics=("arbitrary", "arbitrary"),
            vmem_limit_bytes=VMEM_LIMIT_BYTES),
        name=f"mix_l{layer}",
    )(*args)


def _pad_rows_front(a, rows):
    pad = [(0, 0)] * a.ndim
    pad[-2] = (rows - a.shape[-2], 0)
    return jnp.pad(a, pad)


def _trunk(x, st_b, st_c, st_h, st_d, prep, ffn_weights, depth, emit_v):
    (normg, w_in_bf, wd_bf, w_out_bf, small) = prep
    n, seq, _ = x.shape
    seq_pad = -(-seq // SSD_BLOCK) * SSD_BLOCK
    sb = _pad_rows_front(st_b, HALO_B)
    sc = _pad_rows_front(st_c, HALO_CD)
    sd = _pad_rows_front(st_d, HALO_CD)
    sh = st_h.reshape(st_h.shape[0], n, 4, LANES, LANES)
    x2d = x.reshape(n * seq, D_MODEL)
    casting = not isinstance(ffn_weights, dict)
    cast_weights = {}

    def ffn(x2d, l, which, *extra):
        if casting:
            y, cast_weights[(l, which)] = _ffn(x2d, normg, ffn_weights, l, which, *extra)
            return y
        return _ffn(x2d, normg, ffn_weights[(l, which)], l, which, *extra)

    new = []
    for l in range(depth):
        x2d = ffn(x2d, l, 0)
        xm = x2d.reshape(n, seq, D_MODEL)
        if seq_pad != seq:
            xm = jnp.pad(xm, ((0, 0), (0, seq_pad - seq), (0, 0)))
        outs = _mix(xm, normg, w_in_bf, wd_bf, small, (sb, sc, sh, sd), l,
                    min(seq, MIX_TILE_ROWS), emit_v)
        mixed = outs[0][:, :seq].reshape(n * seq, D_MODEL)
        x2d = ffn(x2d, l, 1, mixed, w_out_bf)
        new.append(outs[1:])
    nb = jnp.stack([s[0][:, HALO_B - (B_CONV_WIDTH - 1):] for s in new])
    nc = jnp.stack([s[1][:, HALO_CD - (C_CONV_WIDTH - 1):] for s in new])
    nh = jnp.stack([s[2].reshape(n, C_HEADS, C_HEAD_DIM, C_STATE) for s in new])
    nd = jnp.stack([s[3][:, HALO_CD - (D_CONV_WIDTH - 1):] for s in new])
    v = jnp.stack([s[4][:, :seq] for s in new]) if emit_v else None
    return (x2d.reshape(n, seq, D_MODEL), v, nb, nc, nh, nd), cast_weights


def _prepare(norm_g, w_in, w_out, a_ln_g, a_ln_b, a_ws, a_bs,
             b_conv_w, b_conv_b, b_ln_g, b_ln_b, c_conv_w, c_conv_b, c_dt_bias, c_a_log, c_d, c_norm_g,
             d_conv_w):
    depth = norm_g.shape[0]
    normg = norm_g.reshape(depth * 6, 1, D_MODEL)
    w_in_bf = w_in.astype(BF16)
    wd_bf = w_in[:, :, IN_D_START:].astype(BF16)
    row = lambda a: a.reshape(depth, 1, a.shape[-1])
    pad_lanes = lambda a: jnp.pad(a, ((0, 0), (0, LANES - a.shape[-1]))).reshape(depth, 1, LANES)
    small = (row(a_ln_g), row(a_ln_b), a_ws, jnp.swapaxes(a_bs, 1, 2),
             b_conv_w, row(b_conv_b), row(b_ln_g), row(b_ln_b),
             c_conv_w, row(c_conv_b), pad_lanes(c_dt_bias), pad_lanes(c_a_log),
             row(jnp.repeat(c_d, C_HEAD_DIM, axis=-1)), row(c_norm_g),
             d_conv_w)
    return (normg, w_in_bf, wd_bf, w_out.astype(BF16), small)


def kernel(x_prompt, x_sample, cache_conv_b, cache_conv_c, state_ssm, cache_conv_d,
           norm_g, ffn_w_in, ffn_w_out, w_in, w_out, a_ln_g, a_ln_b, a_ws, a_bs,
           b_conv_w, b_conv_b, b_ln_g, b_ln_b, c_conv_w, c_conv_b, c_dt_bias, c_a_log, c_d, c_norm_g,
           d_conv_w):
    depth = norm_g.shape[0]
    prep = _prepare(norm_g, w_in, w_out, a_ln_g, a_ln_b, a_ws, a_bs,
                    b_conv_w, b_conv_b, b_ln_g, b_ln_b, c_conv_w, c_conv_b, c_dt_bias, c_a_log, c_d,
                    c_norm_g, d_conv_w)

    (y_s, s_v, s_b, s_c, s_h, s_d), ffn_bf = _trunk(
        x_sample, cache_conv_b, cache_conv_c, state_ssm, cache_conv_d, prep,
        (ffn_w_in, ffn_w_out), depth, True)

    nb = x_prompt.shape[0]
    zb = jnp.zeros((depth, nb, B_CONV_WIDTH - 1, GROUP_WIDTH), F32)
    zc = jnp.zeros((depth, nb, C_CONV_WIDTH - 1, C_XBC), F32)
    zh = jnp.zeros((depth, nb, C_HEADS, C_HEAD_DIM, C_STATE), F32)
    zd = jnp.zeros((depth, nb, D_CONV_WIDTH - 1, GROUP_WIDTH), F32)
    (y_prompt, _, p_b, p_c, p_h, p_d), _ = _trunk(x_prompt, zb, zc, zh, zd, prep, ffn_bf, depth, False)
    return (y_prompt, y_s, p_b, p_c, p_h, p_d, s_v, s_b, s_c, s_h, s_d)
```

```python
import functools

import numpy as np
import jax
import jax.numpy as jnp
from jax import lax
from jax.experimental import pallas as pl
from jax.experimental.pallas import tpu as pltpu

F32 = jnp.float32
BF16 = jnp.bfloat16

D_MODEL = 2048
D_FF = 5632
GROUP_WIDTH = 512
A_HEADS = 4
GMLP_CHUNK = 128
B_CONV_WIDTH = 31
C_HEADS = 8
C_HEAD_DIM = 64
C_STATE = 128
C_CONV_WIDTH = 4
C_XBC = 1024
D_CONV_WIDTH = 3
SSD_BLOCK = 64
IN_D_START = 3592
NORM_EPS = 1e-6
INV_SQRT2 = float(1.0 / np.float32(np.sqrt(2.0)))

SUBLANES = 8
LANES = 128
HALO_B = 32
HALO_CD = 8
VMEM_LIMIT_BYTES = 58 * 1024 * 1024

P_A, P_B, P_Z, P_XS, P_BC, P_DT, P_D, P_END = 0, 1024, 2048, 2560, 3072, 3584, 3712, 5248

FFN_TILE_ROWS = 512
FFN_TILE_FF = 512
FFN_CAST_TILE_FF = 256
MIX_TILE_ROWS = 256


def _rms(x, g):
    ms = jnp.mean(x * x, axis=-1, keepdims=True)
    return x * lax.rsqrt(ms + NORM_EPS) * g


def _layer_norm(x, g, b):
    mu = jnp.mean(x, axis=-1, keepdims=True)
    xc = x - mu
    var = jnp.mean(xc * xc, axis=-1, keepdims=True)
    return xc * lax.rsqrt(var + NORM_EPS) * g + b


def _silu(x):
    return x * jax.nn.sigmoid(x)


def _dot(a, b):
    return jnp.dot(a, b, preferred_element_type=F32)


def _dot_nt(a, b):
    return lax.dot_general(a, b, (((1,), (1,)), ((), ())), preferred_element_type=F32)


def _ffn_body(nk, fused, cast, *refs):
    refs = list(refs)
    x_ref = refs.pop(0)
    if fused:
        mix_ref, wmix_ref, gmix_ref = refs[:3]
        del refs[:3]
    if cast:
        gin_ref, wg_ref, wu_ref, wo_ref, gout_ref, o_ref, wgu_bf_ref, wo_bf_ref = refs[:8]
        del refs[:8]
    else:
        gin_ref, wgu_ref, wo_ref, gout_ref, o_ref = refs[:5]
        del refs[:5]
    h_scr = refs.pop(0)
    x2_scr = refs.pop(0) if fused else None
    k = pl.program_id(1)

    def step(h, first):
        tf = wo_ref.shape[0]
        if cast:
            wgu = jnp.concatenate([wg_ref[...].astype(BF16), wu_ref[...].astype(BF16)], axis=1)
            wo = wo_ref[...].astype(BF16)
            wgu_bf_ref[...] = wgu
            wo_bf_ref[...] = wo
        else:
            wgu, wo = wgu_ref[...], wo_ref[...]
        gu = _dot(h, wgu)
        grp = FFN_CAST_TILE_FF
        a = jnp.concatenate(
            [_silu(gu[:, 2 * j * grp:(2 * j + 1) * grp]) * gu[:, (2 * j + 1) * grp:(2 * j + 2) * grp]
             for j in range(tf // grp)], axis=1).astype(BF16)
        if first:
            o_ref[...] = _dot(a, wo)
        else:
            o_ref[...] += _dot(a, wo)

    @pl.when(k == 0)
    def _():
        if fused:
            x2 = x_ref[...] + _rms(_dot(mix_ref[...], wmix_ref[...]), gmix_ref[...])
            x2_scr[...] = x2
        else:
            x2 = x_ref[...]
        h = _rms(x2, gin_ref[...]).astype(BF16)
        h_scr[...] = h
        step(h, True)

    @pl.when(k > 0)
    def _():
        step(h_scr[...], False)

    @pl.when(k == nk - 1)
    def _():
        base = x2_scr[...] if fused else x_ref[...]
        o_ref[...] = base + _rms(o_ref[...], 0.5 * gout_ref[...])


def _ffn(x2d, normg, weights, layer, which, mix2d=None, w_mix_bf=None):
    m = x2d.shape[0]
    fused = mix2d is not None
    cast = weights[0].dtype != BF16
    tm = min(FFN_TILE_ROWS, m)
    tf = FFN_CAST_TILE_FF if cast else FFN_TILE_FF
    nk = D_FF // tf
    gi = layer * 6 + (0 if which == 0 else 4)
    go = gi + 1
    row = pl.BlockSpec((tm, D_MODEL), lambda i, k: (i, 0))
    gain = lambda idx: pl.BlockSpec((None, 1, D_MODEL), lambda i, k: (idx, 0, 0))
    col_blk = pl.BlockSpec((D_MODEL, 2 * tf), lambda i, k: (0, k))
    row_blk = pl.BlockSpec((tf, D_MODEL), lambda i, k: (k, 0))
    in_specs = [row]
    args = [x2d]
    scratch = [pltpu.VMEM((tm, D_MODEL), BF16)]
    if fused:
        in_specs += [row,
                     pl.BlockSpec((None, D_MODEL, D_MODEL), lambda i, k: (layer, 0, 0),
                                  pipeline_mode=pl.Buffered(1)),
                     gain(layer * 6 + 3)]
        args += [mix2d, w_mix_bf, normg]
        scratch.append(pltpu.VMEM((tm, D_MODEL), F32))
    in_specs.append(gain(gi))
    args.append(normg)
    out_specs = [row]
    out_shape = [jax.ShapeDtypeStruct((m, D_MODEL), F32)]
    if cast:
        w_in, w_out = weights
        in_specs += [
            pl.BlockSpec((None, None, D_MODEL, tf), lambda i, k: (layer, which, 0, k)),
            pl.BlockSpec((None, None, D_MODEL, tf), lambda i, k: (layer, which, 0, k + nk)),
            pl.BlockSpec((None, None, tf, D_MODEL), lambda i, k: (layer, which, k, 0)),
        ]
        args += [w_in, w_in, w_out]
        out_specs += [col_blk, row_blk]
        out_shape += [jax.ShapeDtypeStruct((D_MODEL, 2 * D_FF), BF16),
                      jax.ShapeDtypeStruct((D_FF, D_MODEL), BF16)]
    else:
        in_specs += [col_blk, row_blk]
        args += list(weights)
    in_specs.append(gain(go))
    args.append(normg)
    outs = pl.pallas_call(
        functools.partial(_ffn_body, nk, fused, cast),
        grid=(m // tm, nk),
        in_specs=in_specs,
        out_specs=out_specs,
        out_shape=out_shape,
        scratch_shapes=scratch,
        compiler_params=pltpu.CompilerParams(
            dimension_semantics=("arbitrary" if cast else "parallel", "arbitrary"),
            vmem_limit_bytes=VMEM_LIMIT_BYTES),
        name=f"ffn_l{layer}_{which}",
    )(*args)
    return (outs[0], tuple(outs[1:])) if cast else outs[0]


def _dwconv(ext_ref, w_ref, b_ref, out_ref, taps, halo, tm, width):
    rows = min(128, tm)
    first = halo - (taps - 1)
    for r0 in range(0, tm, rows):
        for c0 in range(0, width, LANES):
            cs = slice(c0, c0 + LANES)
            acc = None
            for res in range(SUBLANES):
                offs = [j for j in range(first, first + taps) if j % SUBLANES == res]
                if not offs:
                    continue
                lo, hi = offs[0], offs[-1]
                if res == 0:
                    win = ext_ref[r0 + lo:r0 + hi + rows, cs]
                else:
                    span = hi - lo + rows + SUBLANES
                    base = r0 + lo - res
                    win = pltpu.roll(ext_ref[base:base + span, cs], span - res, 0)
                for j in offs:
                    term = w_ref[j - first:j - first + 1, cs] * win[j - lo:j - lo + rows]
                    acc = term if acc is None else acc + term
            if b_ref is not None:
                acc = acc + b_ref[0:1, cs]
            out_ref[r0:r0 + rows, cs] = acc


def _project(x_ref, g2_ref, w_main_ref, w_tail_ref, p_ref):
    h = _rms(x_ref[...], g2_ref[...]).astype(BF16)
    p_ref[:, :P_DT] = _dot(h, w_main_ref[...])
    p_ref[:, P_DT:] = _dot(h, w_tail_ref[...])


def _mixers(tm, valid, p, small, scr, mix_ref, nb_ref, nc_ref, nd_ref, v_ref):
    (alg_ref, alb_ref, aws_ref, absT_ref, bcw_ref, bcb_ref, blg_ref, blb_ref,
     ccw_ref, ccb_ref, cdtb_ref, calog_ref, cdsk_ref, cng_ref, dcw_ref) = small
    extb, extc, extd, s_scr, tmp, yc, dt_scr, a_scr = scr

    pa = p[:, P_A:P_B]
    ga = pa * (lax.erf(pa * INV_SQRT2) + 1.0) * 0.5
    u = ga[:, :GROUP_WIDTH]
    vn = _layer_norm(ga[:, GROUP_WIDTH:], alg_ref[...], alb_ref[...])
    if v_ref is not None:
        v_ref[...] = vn
    vb = vn.astype(BF16)
    tc = min(GMLP_CHUNK, tm)
    rr = lax.broadcasted_iota(jnp.int32, (tc, tc), 0)
    cc = lax.broadcasted_iota(jnp.int32, (tc, tc), 1)
    for hd in range(A_HEADS):
        w = jnp.where(rr >= cc, aws_ref[hd, 0:tc, 0:tc], 0.0).astype(BF16)
        bcol = absT_ref[0:tc, hd:hd + 1]
        for c in range(tm // tc):
            rs = slice(c * tc, (c + 1) * tc)
            ls = slice(hd * LANES, (hd + 1) * LANES)
            s = _dot(w, vb[rs, ls]) + bcol
            mix_ref[rs, ls] = (u[rs, ls] * s).astype(BF16)

    extb[HALO_B:HALO_B + tm, :] = p[:, P_B:P_B + GROUP_WIDTH] * jax.nn.sigmoid(p[:, P_B + GROUP_WIDTH:P_Z])
    _dwconv(extb, bcw_ref, bcb_ref, tmp, B_CONV_WIDTH, HALO_B, tm, GROUP_WIDTH)
    nb_ref[...] = extb[valid:valid + HALO_B, :]
    extb[0:HALO_B, :] = extb[tm:tm + HALO_B, :]
    yb = _silu(_layer_norm(tmp[:, 0:GROUP_WIDTH], blg_ref[...], blb_ref[...]))
    mix_ref[:, GROUP_WIDTH:2 * GROUP_WIDTH] = yb.astype(BF16)

    extc[HALO_CD:HALO_CD + tm, :] = p[:, P_XS:P_DT]
    _dwconv(extc, ccw_ref, ccb_ref, tmp, C_CONV_WIDTH, HALO_CD, tm, C_XBC)
    nc_ref[...] = extc[valid:valid + HALO_CD, :]
    extc[0:HALO_CD, :] = extc[tm:tm + HALO_CD, :]
    tmp[...] = _silu(tmp[...])

    dpre = p[:, P_DT:P_D] + cdtb_ref[...]
    dtv = jnp.maximum(dpre, 0.0) + jnp.log1p(jnp.exp(-jnp.abs(dpre)))
    if valid < tm:
        rowid = lax.broadcasted_iota(jnp.int32, (tm, LANES), 0)
        dtv = jnp.where(rowid < valid, dtv, 0.0)
    dt_scr[...] = dtv
    a_scr[...] = dtv * (-jnp.exp(calog_ref[...]))

    bs = SSD_BLOCK
    tri = (lax.broadcasted_iota(jnp.int32, (bs, bs), 0)
           >= lax.broadcasted_iota(jnp.int32, (bs, bs), 1)).astype(F32)
    lane = lax.broadcasted_iota(jnp.int32, (bs, LANES), 1)
    lane1 = lax.broadcasted_iota(jnp.int32, (1, LANES), 1)
    lane2 = lax.broadcasted_iota(jnp.int32, (2 * bs, LANES), 1)
    row2 = lax.broadcasted_iota(jnp.int32, (2 * bs, LANES), 0)
    blockdiag = ((row2 < bs) & (lane2 < bs)) | ((row2 >= bs) & (lane2 >= bs))
    left = lane < bs
    left1 = lane1 < bs
    causal2 = lax.broadcasted_iota(jnp.int32, (bs, LANES), 0) >= jnp.where(left, lane, lane - bs)
    zblk = jnp.zeros((bs, LANES), F32)

    for c in range(tm // bs):
        rs = slice(c * bs, (c + 1) * bs)
        acum = lax.dot_general(tri, a_scr[rs, :], (((1,), (0,)), ((), ())),
                               precision=lax.Precision.HIGHEST, preferred_element_type=F32)
        dt_blk = dt_scr[rs, :]
        ac_t = jnp.concatenate([acum, acum], axis=0).T
        last = acum[bs - 1:bs, :]
        for g in range(2):
            bg_ = tmp[rs, GROUP_WIDTH + g * C_STATE:GROUP_WIDTH + (g + 1) * C_STATE]
            cg_ = tmp[rs, GROUP_WIDTH + (2 + g) * C_STATE:GROUP_WIDTH + (3 + g) * C_STATE].astype(BF16)
            cb2 = _dot_nt(cg_, jnp.concatenate([bg_, bg_], axis=0).astype(BF16))
            bg_t = jnp.concatenate([bg_, zblk], axis=0).T.astype(BF16)
            for jj in range(2):
                j = g * 2 + jj
                ha, hb = 2 * j, 2 * j + 1
                ls = slice(j * LANES, (j + 1) * LANES)
                acol2 = jnp.where(left, acum[:, ha:ha + 1], acum[:, hb:hb + 1])
                arow2 = jnp.where(left1, ac_t[ha:ha + 1, :], ac_t[hb:hb + 1, :])
                dt2 = jnp.where(left, dt_blk[:, ha:ha + 1], dt_blk[:, hb:hb + 1])
                last2 = jnp.where(left1, last[:, ha:ha + 1], last[:, hb:hb + 1])
                decay = jnp.where(causal2, jnp.exp(acol2 - arow2), 0.0)
                g2 = (cb2 * decay).astype(BF16)
                xs = tmp[rs, ls]
                xdt = xs * dt2
                x2 = jnp.where(blockdiag, jnp.concatenate([xdt, xdt], axis=0), 0.0).astype(BF16)
                sj = s_scr[j]
                y_intra = _dot(g2, x2)
                y_inter = _dot(cg_, sj.astype(BF16)) * jnp.exp(acol2)
                yc[rs, ls] = y_intra + y_inter + cdsk_ref[0:1, ls] * xs
                xw = jnp.concatenate([xdt * jnp.exp(last2 - acol2), zblk], axis=0).astype(BF16)
                s_scr[j] = sj * jnp.exp(last2) + _dot(bg_t, xw)

    yz = yc[...] * _silu(p[:, P_Z:P_XS])
    half = GROUP_WIDTH // 2
    for g in range(2):
        yg = yz[:, g * half:(g + 1) * half]
        ms = jnp.mean(yg * yg, axis=-1, keepdims=True)
        yn = yg * lax.rsqrt(ms + NORM_EPS) * cng_ref[0:1, g * half:(g + 1) * half]
        mix_ref[:, 2 * GROUP_WIDTH + g * half:2 * GROUP_WIDTH + (g + 1) * half] = yn.astype(BF16)

    extd[HALO_CD:HALO_CD + tm, :] = (p[:, P_D + GROUP_WIDTH:P_D + 2 * GROUP_WIDTH]
                                     * p[:, P_D + 2 * GROUP_WIDTH:P_END])
    _dwconv(extd, dcw_ref, None, tmp, D_CONV_WIDTH, HALO_CD, tm, GROUP_WIDTH)
    nd_ref[...] = extd[valid:valid + HALO_CD, :]
    extd[0:HALO_CD, :] = extd[tm:tm + HALO_CD, :]
    mix_ref[:, 3 * GROUP_WIDTH:] = (p[:, P_D:P_D + GROUP_WIDTH] * tmp[:, 0:GROUP_WIDTH]).astype(BF16)


def _mix_body(tm, nt, valid, emit_v, *refs):
    n_in = 23
    x_ref = refs[0]
    g2_ref = refs[1]
    w_main_ref, w_tail_ref = refs[2:4]
    small = refs[4:19]
    stb_ref, stc_ref, sth_ref, std_ref = refs[19:23]
    n_out = 6 if emit_v else 5
    outs = refs[n_in:n_in + n_out]
    mix_ref, nb_ref, nc_ref, nh_ref, nd_ref = outs[:5]
    v_ref = outs[5] if emit_v else None
    scr = refs[n_in + n_out:]
    p_scr = scr[0]
    extb, extc, extd, s_scr = scr[1:5]
    work = scr[1:]
    t = pl.program_id(1)

    @pl.when(t == 0)
    def _():
        extb[0:HALO_B, :] = stb_ref[...]
        extc[0:HALO_CD, :] = stc_ref[...]
        extd[0:HALO_CD, :] = std_ref[...]
        for j in range(4):
            s_scr[j] = sth_ref[j].T

    _project(x_ref, g2_ref, w_main_ref, w_tail_ref, p_scr)
    _mixers(tm, valid, p_scr, small, work, mix_ref, nb_ref, nc_ref, nd_ref, v_ref)

    @pl.when(t == nt - 1)
    def _():
        for j in range(4):
            nh_ref[j] = s_scr[j].T


def _mix(x, normg, w_in_bf, w_tail_bf, small, states, layer, valid, emit_v):
    n, seq, _ = x.shape
    tm = min(MIX_TILE_ROWS, seq)
    assert seq % tm == 0
    nt = seq // tm
    st_b, st_c, st_h, st_d = states
    l = layer
    once = pl.Buffered(1)

    def vec(width):
        return pl.BlockSpec((None, 1, width), lambda b, s: (l, 0, 0), pipeline_mode=once)

    def mat(nrows, width):
        return pl.BlockSpec((None, nrows, width), lambda b, s: (l, 0, 0), pipeline_mode=once)

    in_specs = [pl.BlockSpec((None, tm, D_MODEL), lambda b, s: (b, s, 0))]
    args = [x]
    in_specs += [
        pl.BlockSpec((None, 1, D_MODEL), lambda b, s: (l * 6 + 2, 0, 0), pipeline_mode=once),
        pl.BlockSpec((None, D_MODEL, P_DT), lambda b, s: (l, 0, 0), pipeline_mode=once),
        pl.BlockSpec((None, D_MODEL, P_END - P_DT), lambda b, s: (l, 0, 0), pipeline_mode=once),
        vec(GROUP_WIDTH), vec(GROUP_WIDTH),
        pl.BlockSpec((None, A_HEADS, GMLP_CHUNK, GMLP_CHUNK), lambda b, s: (l, 0, 0, 0),
                     pipeline_mode=once),
        mat(GMLP_CHUNK, A_HEADS),
        mat(B_CONV_WIDTH, GROUP_WIDTH), vec(GROUP_WIDTH), vec(GROUP_WIDTH), vec(GROUP_WIDTH),
        mat(C_CONV_WIDTH, C_XBC), vec(C_XBC), vec(LANES), vec(LANES), vec(GROUP_WIDTH), vec(GROUP_WIDTH),
        mat(D_CONV_WIDTH, GROUP_WIDTH),
        pl.BlockSpec((None, None, HALO_B, GROUP_WIDTH), lambda b, s: (l, b, 0, 0)),
        pl.BlockSpec((None, None, HALO_CD, C_XBC), lambda b, s: (l, b, 0, 0)),
        pl.BlockSpec((None, None, 4, LANES, LANES), lambda b, s: (l, b, 0, 0, 0)),
        pl.BlockSpec((None, None, HALO_CD, GROUP_WIDTH), lambda b, s: (l, b, 0, 0)),
    ]
    args += [normg, w_in_bf, w_tail_bf, *small,
             st_b, st_c, st_h, st_d]
    out_specs = [
        pl.BlockSpec((None, tm, D_MODEL), lambda b, s: (b, s, 0)),
        pl.BlockSpec((None, HALO_B, GROUP_WIDTH), lambda b, s: (b, 0, 0)),
        pl.BlockSpec((None, HALO_CD, C_XBC), lambda b, s: (b, 0, 0)),
        pl.BlockSpec((None, 4, LANES, LANES), lambda b, s: (b, 0, 0, 0)),
        pl.BlockSpec((None, HALO_CD, GROUP_WIDTH), lambda b, s: (b, 0, 0)),
    ]
    out_shape = [
        jax.ShapeDtypeStruct((n, seq, D_MODEL), BF16),
        jax.ShapeDtypeStruct((n, HALO_B, GROUP_WIDTH), F32),
        jax.ShapeDtypeStruct((n, HALO_CD, C_XBC), F32),
        jax.ShapeDtypeStruct((n, 4, LANES, LANES), F32),
        jax.ShapeDtypeStruct((n, HALO_CD, GROUP_WIDTH), F32),
    ]
    if emit_v:
        out_specs.append(pl.BlockSpec((None, tm, GROUP_WIDTH), lambda b, s: (b, s, 0)))
        out_shape.append(jax.ShapeDtypeStruct((n, seq, GROUP_WIDTH), F32))
    scratch = [
        pltpu.VMEM((tm, P_END), F32),
        pltpu.VMEM((HALO_B + tm, GROUP_WIDTH), F32),
        pltpu.VMEM((HALO_CD + tm, C_XBC), F32),
        pltpu.VMEM((HALO_CD + tm, GROUP_WIDTH), F32),
        pltpu.VMEM((4, LANES, LANES), F32),
        pltpu.VMEM((tm, C_XBC), F32),
        pltpu.VMEM((tm, GROUP_WIDTH), F32),
        pltpu.VMEM((tm, LANES), F32),
        pltpu.VMEM((tm, LANES), F32),
    ]
    return pl.pallas_call(
        functools.partial(_mix_body, tm, nt, valid, emit_v),
        grid=(n, nt),
        in_specs=in_specs,
        out_specs=out_specs,
        out_shape=out_shape,
        scratch_shapes=scratch,
        compiler_params=pltpu.CompilerParams(
            dimension_semantics=("arbitrary", "arbitrary"),
            vmem_limit_bytes=VMEM_LIMIT_BYTES),
        name=f"mix_l{layer}",
    )(*args)


def _pad_rows_front(a, rows):
    pad = [(0, 0)] * a.ndim
    pad[-2] = (rows - a.shape[-2], 0)
    return jnp.pad(a, pad)


def _trunk(x, st_b, st_c, st_h, st_d, prep, ffn_weights, depth, emit_v):
    (normg, w_in_bf, w_tail_bf, w_out_bf, small) = prep
    n, seq, _ = x.shape
    seq_pad = -(-seq // SSD_BLOCK) * SSD_BLOCK
    sb = _pad_rows_front(st_b, HALO_B)
    sc = _pad_rows_front(st_c, HALO_CD)
    sd = _pad_rows_front(st_d, HALO_CD)
    sh = st_h.reshape(st_h.shape[0], n, 4, LANES, LANES)
    x2d = x.reshape(n * seq, D_MODEL)
    casting = not isinstance(ffn_weights, dict)
    cast_weights = {}

    def ffn(x2d, l, which, *extra):
        if casting:
            y, cast_weights[(l, which)] = _ffn(x2d, normg, ffn_weights, l, which, *extra)
            return y
        return _ffn(x2d, normg, ffn_weights[(l, which)], l, which, *extra)

    new = []
    for l in range(depth):
        x2d = ffn(x2d, l, 0)
        xm = x2d.reshape(n, seq, D_MODEL)
        if seq_pad != seq:
            xm = jnp.pad(xm, ((0, 0), (0, seq_pad - seq), (0, 0)))
        outs = _mix(xm, normg, w_in_bf, w_tail_bf, small, (sb, sc, sh, sd), l,
                    min(seq, MIX_TILE_ROWS), emit_v)
        mixed = outs[0][:, :seq].reshape(n * seq, D_MODEL)
        x2d = ffn(x2d, l, 1, mixed, w_out_bf)
        new.append(outs[1:])
    nb = jnp.stack([s[0][:, HALO_B - (B_CONV_WIDTH - 1):] for s in new])
    nc = jnp.stack([s[1][:, HALO_CD - (C_CONV_WIDTH - 1):] for s in new])
    nh = jnp.stack([s[2].reshape(n, C_HEADS, C_HEAD_DIM, C_STATE) for s in new])
    nd = jnp.stack([s[3][:, HALO_CD - (D_CONV_WIDTH - 1):] for s in new])
    v = jnp.stack([s[4][:, :seq] for s in new]) if emit_v else None
    return (x2d.reshape(n, seq, D_MODEL), v, nb, nc, nh, nd), cast_weights


def _prepare(norm_g, w_in, w_out, a_ln_g, a_ln_b, a_ws, a_bs,
             b_conv_w, b_conv_b, b_ln_g, b_ln_b, c_conv_w, c_conv_b, c_dt_bias, c_a_log, c_d, c_norm_g,
             d_conv_w):
    depth = norm_g.shape[0]
    normg = norm_g.reshape(depth * 6, 1, D_MODEL)
    w_in_bf = w_in.astype(BF16)
    w_tail_bf = jnp.concatenate(
        [w_in_bf[:, :, P_DT:IN_D_START], jnp.zeros((depth, D_MODEL, P_D - IN_D_START), BF16),
         w_in_bf[:, :, IN_D_START:]], axis=-1)
    row = lambda a: a.reshape(depth, 1, a.shape[-1])
    pad_lanes = lambda a: jnp.pad(a, ((0, 0), (0, LANES - a.shape[-1]))).reshape(depth, 1, LANES)
    small = (row(a_ln_g), row(a_ln_b), a_ws, jnp.swapaxes(a_bs, 1, 2),
             b_conv_w, row(b_conv_b), row(b_ln_g), row(b_ln_b),
             c_conv_w, row(c_conv_b), pad_lanes(c_dt_bias), pad_lanes(c_a_log),
             row(jnp.repeat(c_d, C_HEAD_DIM, axis=-1)), row(c_norm_g),
             d_conv_w)
    return (normg, w_in_bf, w_tail_bf, w_out.astype(BF16), small)


def kernel(x_prompt, x_sample, cache_conv_b, cache_conv_c, state_ssm, cache_conv_d,
           norm_g, ffn_w_in, ffn_w_out, w_in, w_out, a_ln_g, a_ln_b, a_ws, a_bs,
           b_conv_w, b_conv_b, b_ln_g, b_ln_b, c_conv_w, c_conv_b, c_dt_bias, c_a_log, c_d, c_norm_g,
           d_conv_w):
    depth = norm_g.shape[0]
    prep = _prepare(norm_g, w_in, w_out, a_ln_g, a_ln_b, a_ws, a_bs,
                    b_conv_w, b_conv_b, b_ln_g, b_ln_b, c_conv_w, c_conv_b, c_dt_bias, c_a_log, c_d,
                    c_norm_g, d_conv_w)

    (y_s, s_v, s_b, s_c, s_h, s_d), ffn_bf = _trunk(
        x_sample, cache_conv_b, cache_conv_c, state_ssm, cache_conv_d, prep,
        (ffn_w_in, ffn_w_out), depth, True)

    nb = x_prompt.shape[0]
    zb = jnp.zeros((depth, nb, B_CONV_WIDTH - 1, GROUP_WIDTH), F32)
    zc = jnp.zeros((depth, nb, C_CONV_WIDTH - 1, C_XBC), F32)
    zh = jnp.zeros((depth, nb, C_HEADS, C_HEAD_DIM, C_STATE), F32)
    zd = jnp.zeros((depth, nb, D_CONV_WIDTH - 1, GROUP_WIDTH), F32)
    (y_prompt, _, p_b, p_c, p_h, p_d), _ = _trunk(x_prompt, zb, zc, zh, zd, prep, ffn_bf, depth, False)
    return (y_prompt, y_s, p_b, p_c, p_h, p_d, s_v, s_b, s_c, s_h, s_d)
```

```python
import functools

import numpy as np
import jax
import jax.numpy as jnp
from jax import lax
from jax.experimental import pallas as pl
from jax.experimental.pallas import tpu as pltpu

F32 = jnp.float32
BF16 = jnp.bfloat16

D_MODEL = 2048
D_FF = 5632
GROUP_WIDTH = 512
A_HEADS = 4
GMLP_CHUNK = 128
B_CONV_WIDTH = 31
C_HEADS = 8
C_HEAD_DIM = 64
C_STATE = 128
C_CONV_WIDTH = 4
C_XBC = 1024
D_CONV_WIDTH = 3
SSD_BLOCK = 64
IN_D_START = 3592
NORM_EPS = 1e-6
INV_SQRT2 = float(1.0 / np.float32(np.sqrt(2.0)))

SUBLANES = 8
LANES = 128
HALO_B = 32
HALO_CD = 8
VMEM_LIMIT_BYTES = 58 * 1024 * 1024

P_A, P_B, P_Z, P_XS, P_BC, P_DT, P_D, P_END = 0, 1024, 2048, 2560, 3072, 3584, 3712, 5248

FFN_TILE_ROWS = 512
FFN_TILE_FF = 512
FFN_CAST_TILE_FF = 256
MIX_TILE_ROWS = 512


def _rms(x, g):
    ms = jnp.mean(x * x, axis=-1, keepdims=True)
    return x * lax.rsqrt(ms + NORM_EPS) * g


def _layer_norm(x, g, b):
    mu = jnp.mean(x, axis=-1, keepdims=True)
    xc = x - mu
    var = jnp.mean(xc * xc, axis=-1, keepdims=True)
    return xc * lax.rsqrt(var + NORM_EPS) * g + b


def _silu(x):
    return x * jax.nn.sigmoid(x)


def _dot(a, b):
    return jnp.dot(a, b, preferred_element_type=F32)


def _dot_nt(a, b):
    return lax.dot_general(a, b, (((1,), (1,)), ((), ())), preferred_element_type=F32)


def _ffn_body(nk, fused, cast, *refs):
    refs = list(refs)
    x_ref = refs.pop(0)
    if fused:
        mix_ref, wmix_ref, gmix_ref = refs[:3]
        del refs[:3]
    if cast:
        gin_ref, wg_ref, wu_ref, wo_ref, gout_ref, o_ref, wgu_bf_ref, wo_bf_ref = refs[:8]
        del refs[:8]
    else:
        gin_ref, wgu_ref, wo_ref, gout_ref, o_ref = refs[:5]
        del refs[:5]
    h_scr = refs.pop(0)
    x2_scr = refs.pop(0) if fused else None
    k = pl.program_id(1)

    def step(h, first):
        tf = wo_ref.shape[0]
        if cast:
            wgu = jnp.concatenate([wg_ref[...].astype(BF16), wu_ref[...].astype(BF16)], axis=1)
            wo = wo_ref[...].astype(BF16)
            wgu_bf_ref[...] = wgu
            wo_bf_ref[...] = wo
        else:
            wgu, wo = wgu_ref[...], wo_ref[...]
        gu = _dot(h, wgu)
        grp = FFN_CAST_TILE_FF
        a = jnp.concatenate(
            [_silu(gu[:, 2 * j * grp:(2 * j + 1) * grp]) * gu[:, (2 * j + 1) * grp:(2 * j + 2) * grp]
             for j in range(tf // grp)], axis=1).astype(BF16)
        if first:
            o_ref[...] = _dot(a, wo)
        else:
            o_ref[...] += _dot(a, wo)

    @pl.when(k == 0)
    def _():
        if fused:
            x2 = x_ref[...] + _rms(_dot(mix_ref[...], wmix_ref[...]), gmix_ref[...])
            x2_scr[...] = x2
        else:
            x2 = x_ref[...]
        h = _rms(x2, gin_ref[...]).astype(BF16)
        h_scr[...] = h
        step(h, True)

    @pl.when(k > 0)
    def _():
        step(h_scr[...], False)

    @pl.when(k == nk - 1)
    def _():
        base = x2_scr[...] if fused else x_ref[...]
        o_ref[...] = base + _rms(o_ref[...], 0.5 * gout_ref[...])


def _ffn(x2d, normg, weights, layer, which, mix2d=None, w_mix_bf=None):
    m = x2d.shape[0]
    fused = mix2d is not None
    cast = weights[0].dtype != BF16
    tm = min(FFN_TILE_ROWS, m)
    tf = FFN_CAST_TILE_FF if cast else FFN_TILE_FF
    nk = D_FF // tf
    gi = layer * 6 + (0 if which == 0 else 4)
    go = gi + 1
    row = pl.BlockSpec((tm, D_MODEL), lambda i, k: (i, 0))
    gain = lambda idx: pl.BlockSpec((None, 1, D_MODEL), lambda i, k: (idx, 0, 0))
    col_blk = pl.BlockSpec((D_MODEL, 2 * tf), lambda i, k: (0, k))
    row_blk = pl.BlockSpec((tf, D_MODEL), lambda i, k: (k, 0))
    in_specs = [row]
    args = [x2d]
    scratch = [pltpu.VMEM((tm, D_MODEL), BF16)]
    if fused:
        in_specs += [row,
                     pl.BlockSpec((None, D_MODEL, D_MODEL), lambda i, k: (layer, 0, 0),
                                  pipeline_mode=pl.Buffered(1)),
                     gain(layer * 6 + 3)]
        args += [mix2d, w_mix_bf, normg]
        scratch.append(pltpu.VMEM((tm, D_MODEL), F32))
    in_specs.append(gain(gi))
    args.append(normg)
    out_specs = [row]
    out_shape = [jax.ShapeDtypeStruct((m, D_MODEL), F32)]
    if cast:
        w_in, w_out = weights
        in_specs += [
            pl.BlockSpec((None, None, D_MODEL, tf), lambda i, k: (layer, which, 0, k)),
            pl.BlockSpec((None, None, D_MODEL, tf), lambda i, k: (layer, which, 0, k + nk)),
            pl.BlockSpec((None, None, tf, D_MODEL), lambda i, k: (layer, which, k, 0)),
        ]
        args += [w_in, w_in, w_out]
        out_specs += [col_blk, row_blk]
        out_shape += [jax.ShapeDtypeStruct((D_MODEL, 2 * D_FF), BF16),
                      jax.ShapeDtypeStruct((D_FF, D_MODEL), BF16)]
    else:
        in_specs += [col_blk, row_blk]
        args += list(weights)
    in_specs.append(gain(go))
    args.append(normg)
    outs = pl.pallas_call(
        functools.partial(_ffn_body, nk, fused, cast),
        grid=(m // tm, nk),
        in_specs=in_specs,
        out_specs=out_specs,
        out_shape=out_shape,
        scratch_shapes=scratch,
        compiler_params=pltpu.CompilerParams(
            dimension_semantics=("arbitrary" if cast else "parallel", "arbitrary"),
            vmem_limit_bytes=VMEM_LIMIT_BYTES),
        name=f"ffn_l{layer}_{which}",
    )(*args)
    return (outs[0], tuple(outs[1:])) if cast else outs[0]


def _dwconv(ext_ref, w_ref, b_ref, out_ref, taps, halo, tm, width):
    rows = min(128, tm)
    first = halo - (taps - 1)
    for r0 in range(0, tm, rows):
        for c0 in range(0, width, LANES):
            cs = slice(c0, c0 + LANES)
            acc = None
            for res in range(SUBLANES):
                offs = [j for j in range(first, first + taps) if j % SUBLANES == res]
                if not offs:
                    continue
                lo, hi = offs[0], offs[-1]
                if res == 0:
                    win = ext_ref[r0 + lo:r0 + hi + rows, cs]
                else:
                    span = hi - lo + rows + SUBLANES
                    base = r0 + lo - res
                    win = pltpu.roll(ext_ref[base:base + span, cs], span - res, 0)
                for j in offs:
                    term = w_ref[j - first:j - first + 1, cs] * win[j - lo:j - lo + rows]
                    acc = term if acc is None else acc + term
            if b_ref is not None:
                acc = acc + b_ref[0:1, cs]
            out_ref[r0:r0 + rows, cs] = acc


def _project(x_ref, g2_ref, w_main_ref, w_tail_ref, p_ref):
    h = _rms(x_ref[...], g2_ref[...]).astype(BF16)
    p_ref[:, :P_DT] = _dot(h, w_main_ref[...])
    p_ref[:, P_DT:] = _dot(h, w_tail_ref[...])


def _mixers(tm, valid, p, small, scr, mix_ref, nb_ref, nc_ref, nd_ref, v_ref):
    (alg_ref, alb_ref, aws_ref, absT_ref, bcw_ref, bcb_ref, blg_ref, blb_ref,
     ccw_ref, ccb_ref, cdtb_ref, calog_ref, cdsk_ref, cng_ref, dcw_ref) = small
    extb, extc, extd, s_scr, tmp, yc, dt_scr, a_scr = scr

    pa = p[:, P_A:P_B]
    ga = pa * (lax.erf(pa * INV_SQRT2) + 1.0) * 0.5
    u = ga[:, :GROUP_WIDTH]
    vn = _layer_norm(ga[:, GROUP_WIDTH:], alg_ref[...], alb_ref[...])
    if v_ref is not None:
        v_ref[...] = vn
    vb = vn.astype(BF16)
    tc = min(GMLP_CHUNK, tm)
    rr = lax.broadcasted_iota(jnp.int32, (tc, tc), 0)
    cc = lax.broadcasted_iota(jnp.int32, (tc, tc), 1)
    for hd in range(A_HEADS):
        w = jnp.where(rr >= cc, aws_ref[hd, 0:tc, 0:tc], 0.0).astype(BF16)
        bcol = absT_ref[0:tc, hd:hd + 1]
        for c in range(tm // tc):
            rs = slice(c * tc, (c + 1) * tc)
            ls = slice(hd * LANES, (hd + 1) * LANES)
            s = _dot(w, vb[rs, ls]) + bcol
            mix_ref[rs, ls] = (u[rs, ls] * s).astype(BF16)

    extb[HALO_B:HALO_B + tm, :] = p[:, P_B:P_B + GROUP_WIDTH] * jax.nn.sigmoid(p[:, P_B + GROUP_WIDTH:P_Z])
    _dwconv(extb, bcw_ref, bcb_ref, tmp, B_CONV_WIDTH, HALO_B, tm, GROUP_WIDTH)
    nb_ref[...] = extb[valid:valid + HALO_B, :]
    extb[0:HALO_B, :] = extb[tm:tm + HALO_B, :]
    yb = _silu(_layer_norm(tmp[:, 0:GROUP_WIDTH], blg_ref[...], blb_ref[...]))
    mix_ref[:, GROUP_WIDTH:2 * GROUP_WIDTH] = yb.astype(BF16)

    extc[HALO_CD:HALO_CD + tm, :] = p[:, P_XS:P_DT]
    _dwconv(extc, ccw_ref, ccb_ref, tmp, C_CONV_WIDTH, HALO_CD, tm, C_XBC)
    nc_ref[...] = extc[valid:valid + HALO_CD, :]
    extc[0:HALO_CD, :] = extc[tm:tm + HALO_CD, :]
    tmp[...] = _silu(tmp[...])

    dpre = p[:, P_DT:P_D] + cdtb_ref[...]
    dtv = jnp.maximum(dpre, 0.0) + jnp.log1p(jnp.exp(-jnp.abs(dpre)))
    if valid < tm:
        rowid = lax.broadcasted_iota(jnp.int32, (tm, LANES), 0)
        dtv = jnp.where(rowid < valid, dtv, 0.0)
    dt_scr[...] = dtv
    a_scr[...] = dtv * (-jnp.exp(calog_ref[...]))

    bs = SSD_BLOCK
    tri = (lax.broadcasted_iota(jnp.int32, (bs, bs), 0)
           >= lax.broadcasted_iota(jnp.int32, (bs, bs), 1)).astype(F32)
    lane = lax.broadcasted_iota(jnp.int32, (bs, LANES), 1)
    lane1 = lax.broadcasted_iota(jnp.int32, (1, LANES), 1)
    lane2 = lax.broadcasted_iota(jnp.int32, (2 * bs, LANES), 1)
    row2 = lax.broadcasted_iota(jnp.int32, (2 * bs, LANES), 0)
    blockdiag = ((row2 < bs) & (lane2 < bs)) | ((row2 >= bs) & (lane2 >= bs))
    left = lane < bs
    left1 = lane1 < bs
    causal2 = lax.broadcasted_iota(jnp.int32, (bs, LANES), 0) >= jnp.where(left, lane, lane - bs)
    zblk = jnp.zeros((bs, LANES), F32)

    for c in range(tm // bs):
        rs = slice(c * bs, (c + 1) * bs)
        acum = lax.dot_general(tri, a_scr[rs, :], (((1,), (0,)), ((), ())),
                               precision=lax.Precision.HIGHEST, preferred_element_type=F32)
        dt_blk = dt_scr[rs, :]
        ac_t = jnp.concatenate([acum, acum], axis=0).T
        last = acum[bs - 1:bs, :]
        for g in range(2):
            bg_ = tmp[rs, GROUP_WIDTH + g * C_STATE:GROUP_WIDTH + (g + 1) * C_STATE]
            cg_ = tmp[rs, GROUP_WIDTH + (2 + g) * C_STATE:GROUP_WIDTH + (3 + g) * C_STATE].astype(BF16)
            cb2 = _dot_nt(cg_, jnp.concatenate([bg_, bg_], axis=0).astype(BF16))
            bg_t = jnp.concatenate([bg_, zblk], axis=0).T.astype(BF16)
            for jj in range(2):
                j = g * 2 + jj
                ha, hb = 2 * j, 2 * j + 1
                ls = slice(j * LANES, (j + 1) * LANES)
                acol2 = jnp.where(left, acum[:, ha:ha + 1], acum[:, hb:hb + 1])
                arow2 = jnp.where(left1, ac_t[ha:ha + 1, :], ac_t[hb:hb + 1, :])
                dt2 = jnp.where(left, dt_blk[:, ha:ha + 1], dt_blk[:, hb:hb + 1])
                last2 = jnp.where(left1, last[:, ha:ha + 1], last[:, hb:hb + 1])
                decay = jnp.where(causal2, jnp.exp(acol2 - arow2), 0.0)
                g2 = (cb2 * decay).astype(BF16)
                xs = tmp[rs, ls]
                xdt = xs * dt2
                x2 = jnp.where(blockdiag, jnp.concatenate([xdt, xdt], axis=0), 0.0).astype(BF16)
                sj = s_scr[j]
                y_intra = _dot(g2, x2)
                y_inter = _dot(cg_, sj.astype(BF16)) * jnp.exp(acol2)
                yc[rs, ls] = y_intra + y_inter + cdsk_ref[0:1, ls] * xs
                xw = jnp.concatenate([xdt * jnp.exp(last2 - acol2), zblk], axis=0).astype(BF16)
                s_scr[j] = sj * jnp.exp(last2) + _dot(bg_t, xw)

    yz = yc[...] * _silu(p[:, P_Z:P_XS])
    half = GROUP_WIDTH // 2
    for g in range(2):
        yg = yz[:, g * half:(g + 1) * half]
        ms = jnp.mean(yg * yg, axis=-1, keepdims=True)
        yn = yg * lax.rsqrt(ms + NORM_EPS) * cng_ref[0:1, g * half:(g + 1) * half]
        mix_ref[:, 2 * GROUP_WIDTH + g * half:2 * GROUP_WIDTH + (g + 1) * half] = yn.astype(BF16)

    extd[HALO_CD:HALO_CD + tm, :] = (p[:, P_D + GROUP_WIDTH:P_D + 2 * GROUP_WIDTH]
                                     * p[:, P_D + 2 * GROUP_WIDTH:P_END])
    _dwconv(extd, dcw_ref, None, tmp, D_CONV_WIDTH, HALO_CD, tm, GROUP_WIDTH)
    nd_ref[...] = extd[valid:valid + HALO_CD, :]
    extd[0:HALO_CD, :] = extd[tm:tm + HALO_CD, :]
    mix_ref[:, 3 * GROUP_WIDTH:] = (p[:, P_D:P_D + GROUP_WIDTH] * tmp[:, 0:GROUP_WIDTH]).astype(BF16)


def _mix_body(tm, nt, valid, emit_v, *refs):
    n_in = 23
    x_ref = refs[0]
    g2_ref = refs[1]
    w_main_ref, w_tail_ref = refs[2:4]
    small = refs[4:19]
    stb_ref, stc_ref, sth_ref, std_ref = refs[19:23]
    n_out = 6 if emit_v else 5
    outs = refs[n_in:n_in + n_out]
    mix_ref, nb_ref, nc_ref, nh_ref, nd_ref = outs[:5]
    v_ref = outs[5] if emit_v else None
    scr = refs[n_in + n_out:]
    p_scr = scr[0]
    extb, extc, extd, s_scr = scr[1:5]
    work = scr[1:]
    t = pl.program_id(1)

    @pl.when(t == 0)
    def _():
        extb[0:HALO_B, :] = stb_ref[...]
        extc[0:HALO_CD, :] = stc_ref[...]
        extd[0:HALO_CD, :] = std_ref[...]
        for j in range(4):
            s_scr[j] = sth_ref[j].T

    _project(x_ref, g2_ref, w_main_ref, w_tail_ref, p_scr)
    _mixers(tm, valid, p_scr, small, work, mix_ref, nb_ref, nc_ref, nd_ref, v_ref)

    @pl.when(t == nt - 1)
    def _():
        for j in range(4):
            nh_ref[j] = s_scr[j].T


def _mix(x, normg, w_in_bf, w_tail_bf, small, states, layer, valid, emit_v):
    n, seq, _ = x.shape
    tm = min(MIX_TILE_ROWS, seq)
    assert seq % tm == 0
    nt = seq // tm
    st_b, st_c, st_h, st_d = states
    l = layer
    once = pl.Buffered(1)

    def vec(width):
        return pl.BlockSpec((None, 1, width), lambda b, s: (l, 0, 0), pipeline_mode=once)

    def mat(nrows, width):
        return pl.BlockSpec((None, nrows, width), lambda b, s: (l, 0, 0), pipeline_mode=once)

    in_specs = [pl.BlockSpec((None, tm, D_MODEL), lambda b, s: (b, s, 0))]
    args = [x]
    in_specs += [
        pl.BlockSpec((None, 1, D_MODEL), lambda b, s: (l * 6 + 2, 0, 0), pipeline_mode=once),
        pl.BlockSpec((None, D_MODEL, P_DT), lambda b, s: (l, 0, 0), pipeline_mode=once),
        pl.BlockSpec((None, D_MODEL, P_END - P_DT), lambda b, s: (l, 0, 0), pipeline_mode=once),
        vec(GROUP_WIDTH), vec(GROUP_WIDTH),
        pl.BlockSpec((None, A_HEADS, GMLP_CHUNK, GMLP_CHUNK), lambda b, s: (l, 0, 0, 0),
                     pipeline_mode=once),
        mat(GMLP_CHUNK, A_HEADS),
        mat(B_CONV_WIDTH, GROUP_WIDTH), vec(GROUP_WIDTH), vec(GROUP_WIDTH), vec(GROUP_WIDTH),
        mat(C_CONV_WIDTH, C_XBC), vec(C_XBC), vec(LANES), vec(LANES), vec(GROUP_WIDTH), vec(GROUP_WIDTH),
        mat(D_CONV_WIDTH, GROUP_WIDTH),
        pl.BlockSpec((None, None, HALO_B, GROUP_WIDTH), lambda b, s: (l, b, 0, 0)),
        pl.BlockSpec((None, None, HALO_CD, C_XBC), lambda b, s: (l, b, 0, 0)),
        pl.BlockSpec((None, None, 4, LANES, LANES), lambda b, s: (l, b, 0, 0, 0)),
        pl.BlockSpec((None, None, HALO_CD, GROUP_WIDTH), lambda b, s: (l, b, 0, 0)),
    ]
    args += [normg, w_in_bf, w_tail_bf, *small,
             st_b, st_c, st_h, st_d]
    out_specs = [
        pl.BlockSpec((None, tm, D_MODEL), lambda b, s: (b, s, 0)),
        pl.BlockSpec((None, HALO_B, GROUP_WIDTH), lambda b, s: (b, 0, 0)),
        pl.BlockSpec((None, HALO_CD, C_XBC), lambda b, s: (b, 0, 0)),
        pl.BlockSpec((None, 4, LANES, LANES), lambda b, s: (b, 0, 0, 0)),
        pl.BlockSpec((None, HALO_CD, GROUP_WIDTH), lambda b, s: (b, 0, 0)),
    ]
    out_shape = [
        jax.ShapeDtypeStruct((n, seq, D_MODEL), BF16),
        jax.ShapeDtypeStruct((n, HALO_B, GROUP_WIDTH), F32),
        jax.ShapeDtypeStruct((n, HALO_CD, C_XBC), F32),
        jax.ShapeDtypeStruct((n, 4, LANES, LANES), F32),
        jax.ShapeDtypeStruct((n, HALO_CD, GROUP_WIDTH), F32),
    ]
    if emit_v:
        out_specs.append(pl.BlockSpec((None, tm, GROUP_WIDTH), lambda b, s: (b, s, 0)))
        out_shape.append(jax.ShapeDtypeStruct((n, seq, GROUP_WIDTH), F32))
    scratch = [
        pltpu.VMEM((tm, P_END), F32),
        pltpu.VMEM((HALO_B + tm, GROUP_WIDTH), F32),
        pltpu.VMEM((HALO_CD + tm, C_XBC), F32),
        pltpu.VMEM((HALO_CD + tm, GROUP_WIDTH), F32),
        pltpu.VMEM((4, LANES, LANES), F32),
        pltpu.VMEM((tm, C_XBC), F32),
        pltpu.VMEM((tm, GROUP_WIDTH), F32),
        pltpu.VMEM((tm, LANES), F32),
        pltpu.VMEM((tm, LANES), F32),
    ]
    return pl.pallas_call(
        functools.partial(_mix_body, tm, nt, valid, emit_v),
        grid=(n, nt),
        in_specs=in_specs,
        out_specs=out_specs,
        out_shape=out_shape,
        scratch_shapes=scratch,
        compiler_params=pltpu.CompilerParams(
            dimension_semantics=("arbitrary", "arbitrary"),
            vmem_limit_bytes=VMEM_LIMIT_BYTES),
        name=f"mix_l{layer}",
    )(*args)


def _pad_rows_front(a, rows):
    pad = [(0, 0)] * a.ndim
    pad[-2] = (rows - a.shape[-2], 0)
    return jnp.pad(a, pad)


def _trunk(x, st_b, st_c, st_h, st_d, prep, ffn_weights, depth, emit_v):
    (normg, w_in_bf, w_tail_bf, w_out_bf, small) = prep
    n, seq, _ = x.shape
    seq_pad = -(-seq // SSD_BLOCK) * SSD_BLOCK
    sb = _pad_rows_front(st_b, HALO_B)
    sc = _pad_rows_front(st_c, HALO_CD)
    sd = _pad_rows_front(st_d, HALO_CD)
    sh = st_h.reshape(st_h.shape[0], n, 4, LANES, LANES)
    x2d = x.reshape(n * seq, D_MODEL)
    casting = not isinstance(ffn_weights, dict)
    cast_weights = {}

    def ffn(x2d, l, which, *extra):
        if casting:
            y, cast_weights[(l, which)] = _ffn(x2d, normg, ffn_weights, l, which, *extra)
            return y
        return _ffn(x2d, normg, ffn_weights[(l, which)], l, which, *extra)

    new = []
    for l in range(depth):
        x2d = ffn(x2d, l, 0)
        xm = x2d.reshape(n, seq, D_MODEL)
        if seq_pad != seq:
            xm = jnp.pad(xm, ((0, 0), (0, seq_pad - seq), (0, 0)))
        outs = _mix(xm, normg, w_in_bf, w_tail_bf, small, (sb, sc, sh, sd), l,
                    min(seq, MIX_TILE_ROWS), emit_v)
        mixed = outs[0][:, :seq].reshape(n * seq, D_MODEL)
        x2d = ffn(x2d, l, 1, mixed, w_out_bf)
        new.append(outs[1:])
    nb = jnp.stack([s[0][:, HALO_B - (B_CONV_WIDTH - 1):] for s in new])
    nc = jnp.stack([s[1][:, HALO_CD - (C_CONV_WIDTH - 1):] for s in new])
    nh = jnp.stack([s[2].reshape(n, C_HEADS, C_HEAD_DIM, C_STATE) for s in new])
    nd = jnp.stack([s[3][:, HALO_CD - (D_CONV_WIDTH - 1):] for s in new])
    v = jnp.stack([s[4][:, :seq] for s in new]) if emit_v else None
    return (x2d.reshape(n, seq, D_MODEL), v, nb, nc, nh, nd), cast_weights


def _prepare(norm_g, w_in, w_out, a_ln_g, a_ln_b, a_ws, a_bs,
             b_conv_w, b_conv_b, b_ln_g, b_ln_b, c_conv_w, c_conv_b, c_dt_bias, c_a_log, c_d, c_norm_g,
             d_conv_w):
    depth = norm_g.shape[0]
    normg = norm_g.reshape(depth * 6, 1, D_MODEL)
    w_in_bf = w_in.astype(BF16)
    w_tail_bf = jnp.concatenate(
        [w_in_bf[:, :, P_DT:IN_D_START], jnp.zeros((depth, D_MODEL, P_D - IN_D_START), BF16),
         w_in_bf[:, :, IN_D_START:]], axis=-1)
    row = lambda a: a.reshape(depth, 1, a.shape[-1])
    pad_lanes = lambda a: jnp.pad(a, ((0, 0), (0, LANES - a.shape[-1]))).reshape(depth, 1, LANES)
    small = (row(a_ln_g), row(a_ln_b), a_ws, jnp.swapaxes(a_bs, 1, 2),
             b_conv_w, row(b_conv_b), row(b_ln_g), row(b_ln_b),
             c_conv_w, row(c_conv_b), pad_lanes(c_dt_bias), pad_lanes(c_a_log),
             row(jnp.repeat(c_d, C_HEAD_DIM, axis=-1)), row(c_norm_g),
             d_conv_w)
    return (normg, w_in_bf, w_tail_bf, w_out.astype(BF16), small)


def kernel(x_prompt, x_sample, cache_conv_b, cache_conv_c, state_ssm, cache_conv_d,
           norm_g, ffn_w_in, ffn_w_out, w_in, w_out, a_ln_g, a_ln_b, a_ws, a_bs,
           b_conv_w, b_conv_b, b_ln_g, b_ln_b, c_conv_w, c_conv_b, c_dt_bias, c_a_log, c_d, c_norm_g,
           d_conv_w):
    depth = norm_g.shape[0]
    prep = _prepare(norm_g, w_in, w_out, a_ln_g, a_ln_b, a_ws, a_bs,
                    b_conv_w, b_conv_b, b_ln_g, b_ln_b, c_conv_w, c_conv_b, c_dt_bias, c_a_log, c_d,
                    c_norm_g, d_conv_w)

    (y_s, s_v, s_b, s_c, s_h, s_d), ffn_bf = _trunk(
        x_sample, cache_conv_b, cache_conv_c, state_ssm, cache_conv_d, prep,
        (ffn_w_in, ffn_w_out), depth, True)

    nb = x_prompt.shape[0]
    zb = jnp.zeros((depth, nb, B_CONV_WIDTH - 1, GROUP_WIDTH), F32)
    zc = jnp.zeros((depth, nb, C_CONV_WIDTH - 1, C_XBC), F32)
    zh = jnp.zeros((depth, nb, C_HEADS, C_HEAD_DIM, C_STATE), F32)
    zd = jnp.zeros((depth, nb, D_CONV_WIDTH - 1, GROUP_WIDTH), F32)
    (y_prompt, _, p_b, p_c, p_h, p_d), _ = _trunk(x_prompt, zb, zc, zh, zd, prep, ffn_bf, depth, False)
    return (y_prompt, y_s, p_b, p_c, p_h, p_d, s_v, s_b, s_c, s_h, s_d)
```

```python
import functools

import numpy as np
import jax
import jax.numpy as jnp
from jax import lax
from jax.experimental import pallas as pl
from jax.experimental.pallas import tpu as pltpu

F32 = jnp.float32
BF16 = jnp.bfloat16

D_MODEL = 2048
D_FF = 5632
GROUP_WIDTH = 512
A_HEADS = 4
GMLP_CHUNK = 128
B_CONV_WIDTH = 31
C_HEADS = 8
C_HEAD_DIM = 64
C_STATE = 128
C_CONV_WIDTH = 4
C_XBC = 1024
D_CONV_WIDTH = 3
SSD_BLOCK = 64
IN_D_START = 3592
NORM_EPS = 1e-6
INV_SQRT2 = float(1.0 / np.float32(np.sqrt(2.0)))

SUBLANES = 8
LANES = 128
HALO_B = 32
HALO_CD = 8
VMEM_LIMIT_BYTES = 58 * 1024 * 1024

P_A, P_B, P_Z, P_XS, P_BC, P_DT, P_D, P_END = 0, 1024, 2048, 2560, 3072, 3584, 3712, 5248

FFN_TILE_ROWS = 512
FFN_TILE_FF = 512
FFN_CAST_TILE_FF = 512
MIX_TILE_ROWS = 512


def _rms(x, g):
    ms = jnp.mean(x * x, axis=-1, keepdims=True)
    return x * lax.rsqrt(ms + NORM_EPS) * g


def _layer_norm(x, g, b):
    mu = jnp.mean(x, axis=-1, keepdims=True)
    xc = x - mu
    var = jnp.mean(xc * xc, axis=-1, keepdims=True)
    return xc * lax.rsqrt(var + NORM_EPS) * g + b


def _silu(x):
    return x * jax.nn.sigmoid(x)


def _dot(a, b):
    return jnp.dot(a, b, preferred_element_type=F32)


def _dot_nt(a, b):
    return lax.dot_general(a, b, (((1,), (1,)), ((), ())), preferred_element_type=F32)


def _ffn_body(nk, fused, cast, *refs):
    refs = list(refs)
    x_ref = refs.pop(0)
    if fused:
        mix_ref, wmix_ref, gmix_ref = refs[:3]
        del refs[:3]
    if cast:
        gin_ref, wg_ref, wu_ref, wo_ref, gout_ref, o_ref, wgu_bf_ref, wo_bf_ref = refs[:8]
        del refs[:8]
    else:
        gin_ref, wgu_ref, wo_ref, gout_ref, o_ref = refs[:5]
        del refs[:5]
    h_scr = refs.pop(0)
    x2_scr = refs.pop(0) if fused else None
    k = pl.program_id(1)

    def step(h, first):
        tf = wo_ref.shape[0]
        if cast:
            wgu = jnp.concatenate([wg_ref[...].astype(BF16), wu_ref[...].astype(BF16)], axis=1)
            wo = wo_ref[...].astype(BF16)
            wgu_bf_ref[...] = wgu
            wo_bf_ref[...] = wo
        else:
            wgu, wo = wgu_ref[...], wo_ref[...]
        gu = _dot(h, wgu)
        grp = FFN_CAST_TILE_FF
        a = jnp.concatenate(
            [_silu(gu[:, 2 * j * grp:(2 * j + 1) * grp]) * gu[:, (2 * j + 1) * grp:(2 * j + 2) * grp]
             for j in range(tf // grp)], axis=1).astype(BF16)
        if first:
            o_ref[...] = _dot(a, wo)
        else:
            o_ref[...] += _dot(a, wo)

    @pl.when(k == 0)
    def _():
        if fused:
            x2 = x_ref[...] + _rms(_dot(mix_ref[...], wmix_ref[...]), gmix_ref[...])
            x2_scr[...] = x2
        else:
            x2 = x_ref[...]
        h = _rms(x2, gin_ref[...]).astype(BF16)
        h_scr[...] = h
        step(h, True)

    @pl.when(k > 0)
    def _():
        step(h_scr[...], False)

    @pl.when(k == nk - 1)
    def _():
        base = x2_scr[...] if fused else x_ref[...]
        o_ref[...] = base + _rms(o_ref[...], 0.5 * gout_ref[...])


def _ffn(x2d, normg, weights, layer, which, mix2d=None, w_mix_bf=None):
    m = x2d.shape[0]
    fused = mix2d is not None
    cast = weights[0].dtype != BF16
    tm = min(FFN_TILE_ROWS, m)
    tf = FFN_CAST_TILE_FF if cast else FFN_TILE_FF
    nk = D_FF // tf
    gi = layer * 6 + (0 if which == 0 else 4)
    go = gi + 1
    row = pl.BlockSpec((tm, D_MODEL), lambda i, k: (i, 0))
    gain = lambda idx: pl.BlockSpec((None, 1, D_MODEL), lambda i, k: (idx, 0, 0))
    col_blk = pl.BlockSpec((D_MODEL, 2 * tf), lambda i, k: (0, k))
    row_blk = pl.BlockSpec((tf, D_MODEL), lambda i, k: (k, 0))
    in_specs = [row]
    args = [x2d]
    scratch = [pltpu.VMEM((tm, D_MODEL), BF16)]
    if fused:
        in_specs += [row,
                     pl.BlockSpec((None, D_MODEL, D_MODEL), lambda i, k: (layer, 0, 0),
                                  pipeline_mode=pl.Buffered(1)),
                     gain(layer * 6 + 3)]
        args += [mix2d, w_mix_bf, normg]
        scratch.append(pltpu.VMEM((tm, D_MODEL), F32))
    in_specs.append(gain(gi))
    args.append(normg)
    out_specs = [row]
    out_shape = [jax.ShapeDtypeStruct((m, D_MODEL), F32)]
    if cast:
        w_in, w_out = weights
        in_specs += [
            pl.BlockSpec((None, None, D_MODEL, tf), lambda i, k: (layer, which, 0, k)),
            pl.BlockSpec((None, None, D_MODEL, tf), lambda i, k: (layer, which, 0, k + nk)),
            pl.BlockSpec((None, None, tf, D_MODEL), lambda i, k: (layer, which, k, 0)),
        ]
        args += [w_in, w_in, w_out]
        out_specs += [col_blk, row_blk]
        out_shape += [jax.ShapeDtypeStruct((D_MODEL, 2 * D_FF), BF16),
                      jax.ShapeDtypeStruct((D_FF, D_MODEL), BF16)]
    else:
        in_specs += [col_blk, row_blk]
        args += list(weights)
    in_specs.append(gain(go))
    args.append(normg)
    outs = pl.pallas_call(
        functools.partial(_ffn_body, nk, fused, cast),
        grid=(m // tm, nk),
        in_specs=in_specs,
        out_specs=out_specs,
        out_shape=out_shape,
        scratch_shapes=scratch,
        compiler_params=pltpu.CompilerParams(
            dimension_semantics=("arbitrary" if cast else "parallel", "arbitrary"),
            vmem_limit_bytes=VMEM_LIMIT_BYTES),
        name=f"ffn_l{layer}_{which}",
    )(*args)
    return (outs[0], tuple(outs[1:])) if cast else outs[0]


def _dwconv(ext_ref, w_ref, b_ref, out_ref, taps, halo, tm, width):
    rows = min(128, tm)
    first = halo - (taps - 1)
    for r0 in range(0, tm, rows):
        for c0 in range(0, width, LANES):
            cs = slice(c0, c0 + LANES)
            acc = None
            for res in range(SUBLANES):
                offs = [j for j in range(first, first + taps) if j % SUBLANES == res]
                if not offs:
                    continue
                lo, hi = offs[0], offs[-1]
                if res == 0:
                    win = ext_ref[r0 + lo:r0 + hi + rows, cs]
                else:
                    span = hi - lo + rows + SUBLANES
                    base = r0 + lo - res
                    win = pltpu.roll(ext_ref[base:base + span, cs], span - res, 0)
                for j in offs:
                    term = w_ref[j - first:j - first + 1, cs] * win[j - lo:j - lo + rows]
                    acc = term if acc is None else acc + term
            if b_ref is not None:
                acc = acc + b_ref[0:1, cs]
            out_ref[r0:r0 + rows, cs] = acc


def _project(x_ref, g2_ref, w_main_ref, w_tail_ref, p_ref):
    h = _rms(x_ref[...], g2_ref[...]).astype(BF16)
    p_ref[:, :P_DT] = _dot(h, w_main_ref[...])
    p_ref[:, P_DT:] = _dot(h, w_tail_ref[...])


def _mixers(tm, valid, p, small, scr, mix_ref, nb_ref, nc_ref, nd_ref, v_ref):
    (alg_ref, alb_ref, aws_ref, absT_ref, bcw_ref, bcb_ref, blg_ref, blb_ref,
     ccw_ref, ccb_ref, cdtb_ref, calog_ref, cdsk_ref, cng_ref, dcw_ref) = small
    extb, extc, extd, s_scr, tmp, yc, dt_scr, a_scr = scr

    pa = p[:, P_A:P_B]
    ga = pa * (lax.erf(pa * INV_SQRT2) + 1.0) * 0.5
    u = ga[:, :GROUP_WIDTH]
    vn = _layer_norm(ga[:, GROUP_WIDTH:], alg_ref[...], alb_ref[...])
    if v_ref is not None:
        v_ref[...] = vn
    vb = vn.astype(BF16)
    tc = min(GMLP_CHUNK, tm)
    rr = lax.broadcasted_iota(jnp.int32, (tc, tc), 0)
    cc = lax.broadcasted_iota(jnp.int32, (tc, tc), 1)
    for hd in range(A_HEADS):
        w = jnp.where(rr >= cc, aws_ref[hd, 0:tc, 0:tc], 0.0).astype(BF16)
        bcol = absT_ref[0:tc, hd:hd + 1]
        for c in range(tm // tc):
            rs = slice(c * tc, (c + 1) * tc)
            ls = slice(hd * LANES, (hd + 1) * LANES)
            s = _dot(w, vb[rs, ls]) + bcol
            mix_ref[rs, ls] = (u[rs, ls] * s).astype(BF16)

    extb[HALO_B:HALO_B + tm, :] = p[:, P_B:P_B + GROUP_WIDTH] * jax.nn.sigmoid(p[:, P_B + GROUP_WIDTH:P_Z])
    _dwconv(extb, bcw_ref, bcb_ref, tmp, B_CONV_WIDTH, HALO_B, tm, GROUP_WIDTH)
    nb_ref[...] = extb[valid:valid + HALO_B, :]
    extb[0:HALO_B, :] = extb[tm:tm + HALO_B, :]
    yb = _silu(_layer_norm(tmp[:, 0:GROUP_WIDTH], blg_ref[...], blb_ref[...]))
    mix_ref[:, GROUP_WIDTH:2 * GROUP_WIDTH] = yb.astype(BF16)

    extc[HALO_CD:HALO_CD + tm, :] = p[:, P_XS:P_DT]
    _dwconv(extc, ccw_ref, ccb_ref, tmp, C_CONV_WIDTH, HALO_CD, tm, C_XBC)
    nc_ref[...] = extc[valid:valid + HALO_CD, :]
    extc[0:HALO_CD, :] = extc[tm:tm + HALO_CD, :]
    tmp[...] = _silu(tmp[...])

    dpre = p[:, P_DT:P_D] + cdtb_ref[...]
    dtv = jnp.maximum(dpre, 0.0) + jnp.log1p(jnp.exp(-jnp.abs(dpre)))
    if valid < tm:
        rowid = lax.broadcasted_iota(jnp.int32, (tm, LANES), 0)
        dtv = jnp.where(rowid < valid, dtv, 0.0)
    dt_scr[...] = dtv
    a_scr[...] = dtv * (-jnp.exp(calog_ref[...]))

    bs = SSD_BLOCK
    tri = (lax.broadcasted_iota(jnp.int32, (bs, bs), 0)
           >= lax.broadcasted_iota(jnp.int32, (bs, bs), 1)).astype(F32)
    lane = lax.broadcasted_iota(jnp.int32, (bs, LANES), 1)
    lane1 = lax.broadcasted_iota(jnp.int32, (1, LANES), 1)
    lane2 = lax.broadcasted_iota(jnp.int32, (2 * bs, LANES), 1)
    row2 = lax.broadcasted_iota(jnp.int32, (2 * bs, LANES), 0)
    blockdiag = ((row2 < bs) & (lane2 < bs)) | ((row2 >= bs) & (lane2 >= bs))
    left = lane < bs
    left1 = lane1 < bs
    causal2 = lax.broadcasted_iota(jnp.int32, (bs, LANES), 0) >= jnp.where(left, lane, lane - bs)
    zblk = jnp.zeros((bs, LANES), F32)

    for c in range(tm // bs):
        rs = slice(c * bs, (c + 1) * bs)
        acum = lax.dot_general(tri, a_scr[rs, :], (((1,), (0,)), ((), ())),
                               precision=lax.Precision.HIGHEST, preferred_element_type=F32)
        dt_blk = dt_scr[rs, :]
        ac_t = jnp.concatenate([acum, acum], axis=0).T
        last = acum[bs - 1:bs, :]
        for g in range(2):
            bg_ = tmp[rs, GROUP_WIDTH + g * C_STATE:GROUP_WIDTH + (g + 1) * C_STATE]
            cg_ = tmp[rs, GROUP_WIDTH + (2 + g) * C_STATE:GROUP_WIDTH + (3 + g) * C_STATE].astype(BF16)
            cb2 = _dot_nt(cg_, jnp.concatenate([bg_, bg_], axis=0).astype(BF16))
            bg_t = jnp.concatenate([bg_, zblk], axis=0).T.astype(BF16)
            for jj in range(2):
                j = g * 2 + jj
                ha, hb = 2 * j, 2 * j + 1
                ls = slice(j * LANES, (j + 1) * LANES)
                acol2 = jnp.where(left, acum[:, ha:ha + 1], acum[:, hb:hb + 1])
                arow2 = jnp.where(left1, ac_t[ha:ha + 1, :], ac_t[hb:hb + 1, :])
                dt2 = jnp.where(left, dt_blk[:, ha:ha + 1], dt_blk[:, hb:hb + 1])
                last2 = jnp.where(left1, last[:, ha:ha + 1], last[:, hb:hb + 1])
                decay = jnp.where(causal2, jnp.exp(acol2 - arow2), 0.0)
                g2 = (cb2 * decay).astype(BF16)
                xs = tmp[rs, ls]
                xdt = xs * dt2
                x2 = jnp.where(blockdiag, jnp.concatenate([xdt, xdt], axis=0), 0.0).astype(BF16)
                sj = s_scr[j]
                y_intra = _dot(g2, x2)
                y_inter = _dot(cg_, sj.astype(BF16)) * jnp.exp(acol2)
                yc[rs, ls] = y_intra + y_inter + cdsk_ref[0:1, ls] * xs
                xw = jnp.concatenate([xdt * jnp.exp(last2 - acol2), zblk], axis=0).astype(BF16)
                s_scr[j] = sj * jnp.exp(last2) + _dot(bg_t, xw)

    yz = yc[...] * _silu(p[:, P_Z:P_XS])
    half = GROUP_WIDTH // 2
    for g in range(2):
        yg = yz[:, g * half:(g + 1) * half]
        ms = jnp.mean(yg * yg, axis=-1, keepdims=True)
        yn = yg * lax.rsqrt(ms + NORM_EPS) * cng_ref[0:1, g * half:(g + 1) * half]
        mix_ref[:, 2 * GROUP_WIDTH + g * half:2 * GROUP_WIDTH + (g + 1) * half] = yn.astype(BF16)

    extd[HALO_CD:HALO_CD + tm, :] = (p[:, P_D + GROUP_WIDTH:P_D + 2 * GROUP_WIDTH]
                                     * p[:, P_D + 2 * GROUP_WIDTH:P_END])
    _dwconv(extd, dcw_ref, None, tmp, D_CONV_WIDTH, HALO_CD, tm, GROUP_WIDTH)
    nd_ref[...] = extd[valid:valid + HALO_CD, :]
    extd[0:HALO_CD, :] = extd[tm:tm + HALO_CD, :]
    mix_ref[:, 3 * GROUP_WIDTH:] = (p[:, P_D:P_D + GROUP_WIDTH] * tmp[:, 0:GROUP_WIDTH]).astype(BF16)


def _mix_body(tm, nt, valid, emit_v, *refs):
    n_in = 23
    x_ref = refs[0]
    g2_ref = refs[1]
    w_main_ref, w_tail_ref = refs[2:4]
    small = refs[4:19]
    stb_ref, stc_ref, sth_ref, std_ref = refs[19:23]
    n_out = 6 if emit_v else 5
    outs = refs[n_in:n_in + n_out]
    mix_ref, nb_ref, nc_ref, nh_ref, nd_ref = outs[:5]
    v_ref = outs[5] if emit_v else None
    scr = refs[n_in + n_out:]
    p_scr = scr[0]
    extb, extc, extd, s_scr = scr[1:5]
    work = scr[1:]
    t = pl.program_id(1)

    @pl.when(t == 0)
    def _():
        extb[0:HALO_B, :] = stb_ref[...]
        extc[0:HALO_CD, :] = stc_ref[...]
        extd[0:HALO_CD, :] = std_ref[...]
        for j in range(4):
            s_scr[j] = sth_ref[j].T

    _project(x_ref, g2_ref, w_main_ref, w_tail_ref, p_scr)
    _mixers(tm, valid, p_scr, small, work, mix_ref, nb_ref, nc_ref, nd_ref, v_ref)

    @pl.when(t == nt - 1)
    def _():
        for j in range(4):
            nh_ref[j] = s_scr[j].T


def _mix(x, normg, w_in_bf, w_tail_bf, small, states, layer, valid, emit_v):
    n, seq, _ = x.shape
    tm = min(MIX_TILE_ROWS, seq)
    assert seq % tm == 0
    nt = seq // tm
    st_b, st_c, st_h, st_d = states
    l = layer
    once = pl.Buffered(1)

    def vec(width):
        return pl.BlockSpec((None, 1, width), lambda b, s: (l, 0, 0), pipeline_mode=once)

    def mat(nrows, width):
        return pl.BlockSpec((None, nrows, width), lambda b, s: (l, 0, 0), pipeline_mode=once)

    in_specs = [pl.BlockSpec((None, tm, D_MODEL), lambda b, s: (b, s, 0))]
    args = [x]
    in_specs += [
        pl.BlockSpec((None, 1, D_MODEL), lambda b, s: (l * 6 + 2, 0, 0), pipeline_mode=once),
        pl.BlockSpec((None, D_MODEL, P_DT), lambda b, s: (l, 0, 0), pipeline_mode=once),
        pl.BlockSpec((None, D_MODEL, P_END - P_DT), lambda b, s: (l, 0, 0), pipeline_mode=once),
        vec(GROUP_WIDTH), vec(GROUP_WIDTH),
        pl.BlockSpec((None, A_HEADS, GMLP_CHUNK, GMLP_CHUNK), lambda b, s: (l, 0, 0, 0),
                     pipeline_mode=once),
        mat(GMLP_CHUNK, A_HEADS),
        mat(B_CONV_WIDTH, GROUP_WIDTH), vec(GROUP_WIDTH), vec(GROUP_WIDTH), vec(GROUP_WIDTH),
        mat(C_CONV_WIDTH, C_XBC), vec(C_XBC), vec(LANES), vec(LANES), vec(GROUP_WIDTH), vec(GROUP_WIDTH),
        mat(D_CONV_WIDTH, GROUP_WIDTH),
        pl.BlockSpec((None, None, HALO_B, GROUP_WIDTH), lambda b, s: (l, b, 0, 0)),
        pl.BlockSpec((None, None, HALO_CD, C_XBC), lambda b, s: (l, b, 0, 0)),
        pl.BlockSpec((None, None, 4, LANES, LANES), lambda b, s: (l, b, 0, 0, 0)),
        pl.BlockSpec((None, None, HALO_CD, GROUP_WIDTH), lambda b, s: (l, b, 0, 0)),
    ]
    args += [normg, w_in_bf, w_tail_bf, *small,
             st_b, st_c, st_h, st_d]
    out_specs = [
        pl.BlockSpec((None, tm, D_MODEL), lambda b, s: (b, s, 0)),
        pl.BlockSpec((None, HALO_B, GROUP_WIDTH), lambda b, s: (b, 0, 0)),
        pl.BlockSpec((None, HALO_CD, C_XBC), lambda b, s: (b, 0, 0)),
        pl.BlockSpec((None, 4, LANES, LANES), lambda b, s: (b, 0, 0, 0)),
        pl.BlockSpec((None, HALO_CD, GROUP_WIDTH), lambda b, s: (b, 0, 0)),
    ]
    out_shape = [
        jax.ShapeDtypeStruct((n, seq, D_MODEL), BF16),
        jax.ShapeDtypeStruct((n, HALO_B, GROUP_WIDTH), F32),
        jax.ShapeDtypeStruct((n, HALO_CD, C_XBC), F32),
        jax.ShapeDtypeStruct((n, 4, LANES, LANES), F32),
        jax.ShapeDtypeStruct((n, HALO_CD, GROUP_WIDTH), F32),
    ]
    if emit_v:
        out_specs.append(pl.BlockSpec((None, tm, GROUP_WIDTH), lambda b, s: (b, s, 0)))
        out_shape.append(jax.ShapeDtypeStruct((n, seq, GROUP_WIDTH), F32))
    scratch = [
        pltpu.VMEM((tm, P_END), F32),
        pltpu.VMEM((HALO_B + tm, GROUP_WIDTH), F32),
        pltpu.VMEM((HALO_CD + tm, C_XBC), F32),
        pltpu.VMEM((HALO_CD + tm, GROUP_WIDTH), F32),
        pltpu.VMEM((4, LANES, LANES), F32),
        pltpu.VMEM((tm, C_XBC), F32),
        pltpu.VMEM((tm, GROUP_WIDTH), F32),
        pltpu.VMEM((tm, LANES), F32),
        pltpu.VMEM((tm, LANES), F32),
    ]
    return pl.pallas_call(
        functools.partial(_mix_body, tm, nt, valid, emit_v),
        grid=(n, nt),
        in_specs=in_specs,
        out_specs=out_specs,
        out_shape=out_shape,
        scratch_shapes=scratch,
        compiler_params=pltpu.CompilerParams(
            dimension_semantics=("arbitrary", "arbitrary"),
            vmem_limit_bytes=VMEM_LIMIT_BYTES),
        name=f"mix_l{layer}",
    )(*args)


def _pad_rows_front(a, rows):
    pad = [(0, 0)] * a.ndim
    pad[-2] = (rows - a.shape[-2], 0)
    return jnp.pad(a, pad)


def _trunk(x, st_b, st_c, st_h, st_d, prep, ffn_weights, depth, emit_v):
    (normg, w_in_bf, w_tail_bf, w_out_bf, small) = prep
    n, seq, _ = x.shape
    seq_pad = -(-seq // SSD_BLOCK) * SSD_BLOCK
    sb = _pad_rows_front(st_b, HALO_B)
    sc = _pad_rows_front(st_c, HALO_CD)
    sd = _pad_rows_front(st_d, HALO_CD)
    sh = st_h.reshape(st_h.shape[0], n, 4, LANES, LANES)
    x2d = x.reshape(n * seq, D_MODEL)
    casting = not isinstance(ffn_weights, dict)
    cast_weights = {}

    def ffn(x2d, l, which, *extra):
        if casting:
            y, cast_weights[(l, which)] = _ffn(x2d, normg, ffn_weights, l, which, *extra)
            return y
        return _ffn(x2d, normg, ffn_weights[(l, which)], l, which, *extra)

    new = []
    for l in range(depth):
        x2d = ffn(x2d, l, 0)
        xm = x2d.reshape(n, seq, D_MODEL)
        if seq_pad != seq:
            xm = jnp.pad(xm, ((0, 0), (0, seq_pad - seq), (0, 0)))
        outs = _mix(xm, normg, w_in_bf, w_tail_bf, small, (sb, sc, sh, sd), l,
                    min(seq, MIX_TILE_ROWS), emit_v)
        mixed = outs[0][:, :seq].reshape(n * seq, D_MODEL)
        x2d = ffn(x2d, l, 1, mixed, w_out_bf)
        new.append(outs[1:])
    nb = jnp.stack([s[0][:, HALO_B - (B_CONV_WIDTH - 1):] for s in new])
    nc = jnp.stack([s[1][:, HALO_CD - (C_CONV_WIDTH - 1):] for s in new])
    nh = jnp.stack([s[2].reshape(n, C_HEADS, C_HEAD_DIM, C_STATE) for s in new])
    nd = jnp.stack([s[3][:, HALO_CD - (D_CONV_WIDTH - 1):] for s in new])
    v = jnp.stack([s[4][:, :seq] for s in new]) if emit_v else None
    return (x2d.reshape(n, seq, D_MODEL), v, nb, nc, nh, nd), cast_weights


def _prepare(norm_g, w_in, w_out, a_ln_g, a_ln_b, a_ws, a_bs,
             b_conv_w, b_conv_b, b_ln_g, b_ln_b, c_conv_w, c_conv_b, c_dt_bias, c_a_log, c_d, c_norm_g,
             d_conv_w):
    depth = norm_g.shape[0]
    normg = norm_g.reshape(depth * 6, 1, D_MODEL)
    w_in_bf = w_in.astype(BF16)
    w_tail_bf = jnp.concatenate(
        [w_in_bf[:, :, P_DT:IN_D_START], jnp.zeros((depth, D_MODEL, P_D - IN_D_START), BF16),
         w_in_bf[:, :, IN_D_START:]], axis=-1)
    row = lambda a: a.reshape(depth, 1, a.shape[-1])
    pad_lanes = lambda a: jnp.pad(a, ((0, 0), (0, LANES - a.shape[-1]))).reshape(depth, 1, LANES)
    small = (row(a_ln_g), row(a_ln_b), a_ws, jnp.swapaxes(a_bs, 1, 2),
             b_conv_w, row(b_conv_b), row(b_ln_g), row(b_ln_b),
             c_conv_w, row(c_conv_b), pad_lanes(c_dt_bias), pad_lanes(c_a_log),
             row(jnp.repeat(c_d, C_HEAD_DIM, axis=-1)), row(c_norm_g),
             d_conv_w)
    return (normg, w_in_bf, w_tail_bf, w_out.astype(BF16), small)


def kernel(x_prompt, x_sample, cache_conv_b, cache_conv_c, state_ssm, cache_conv_d,
           norm_g, ffn_w_in, ffn_w_out, w_in, w_out, a_ln_g, a_ln_b, a_ws, a_bs,
           b_conv_w, b_conv_b, b_ln_g, b_ln_b, c_conv_w, c_conv_b, c_dt_bias, c_a_log, c_d, c_norm_g,
           d_conv_w):
    depth = norm_g.shape[0]
    prep = _prepare(norm_g, w_in, w_out, a_ln_g, a_ln_b, a_ws, a_bs,
                    b_conv_w, b_conv_b, b_ln_g, b_ln_b, c_conv_w, c_conv_b, c_dt_bias, c_a_log, c_d,
                    c_norm_g, d_conv_w)

    (y_s, s_v, s_b, s_c, s_h, s_d), ffn_bf = _trunk(
        x_sample, cache_conv_b, cache_conv_c, state_ssm, cache_conv_d, prep,
        (ffn_w_in, ffn_w_out), depth, True)

    nb = x_prompt.shape[0]
    zb = jnp.zeros((depth, nb, B_CONV_WIDTH - 1, GROUP_WIDTH), F32)
    zc = jnp.zeros((depth, nb, C_CONV_WIDTH - 1, C_XBC), F32)
    zh = jnp.zeros((depth, nb, C_HEADS, C_HEAD_DIM, C_STATE), F32)
    zd = jnp.zeros((depth, nb, D_CONV_WIDTH - 1, GROUP_WIDTH), F32)
    (y_prompt, _, p_b, p_c, p_h, p_d), _ = _trunk(x_prompt, zb, zc, zh, zd, prep, ffn_bf, depth, False)
    return (y_prompt, y_s, p_b, p_c, p_h, p_d, s_v, s_b, s_c, s_h, s_d)
```
